```python
import math
import jax, jax.numpy as jnp
from jax import lax
import numpy as np

D_MODEL = 1024
BATCH = 8
SEQ = 2048
DEPTH = 1
DEC_BATCH = 128
DEC_SEQ = 1
PAST_LEN = 16384
PAGE_SIZE = 128

MLA_HEADS = 8
MLA_NOPE = 64
MLA_ROPE = 32
MLA_V = 64
MLA_Q_RANK = 256
MLA_KV_RANK = 128
MLA_WIDTH = MLA_HEADS * MLA_V
MLA_SCALE = (MLA_NOPE + MLA_ROPE) ** -0.5
FOX_HEADS = 8
FOX_HD = 64
FOX_WIDTH = FOX_HEADS * FOX_HD
FOX_SCALE = FOX_HD ** -0.5
PLE_DIM = 256
Q_BLOCK = 128
ROPE_THETA = 10000.0
EPS = 1e-6
NEG = -1e30

SPLITS = (MLA_Q_RANK, MLA_KV_RANK, MLA_ROPE, FOX_WIDTH, FOX_WIDTH, FOX_WIDTH, FOX_HEADS,
          MLA_WIDTH, FOX_WIDTH, D_MODEL, D_MODEL)
IN_WIDTH = MLA_Q_RANK + MLA_KV_RANK + MLA_ROPE + 3 * FOX_WIDTH + FOX_HEADS + MLA_WIDTH + FOX_WIDTH + 2 * D_MODEL

kernel_name = "mla_fox_parallel_gated_decoder_step"


def _rmsnorm(x, g):
    xf = x.astype(jnp.float32)
    inv = lax.rsqrt(jnp.mean(xf * xf, axis=-1, keepdims=True) + EPS)
    return (xf * inv).astype(x.dtype) * g


def _rope(x, pos):
    half = x.shape[-1] // 2
    inv_freq = jnp.exp(-math.log(ROPE_THETA) * jnp.arange(half, dtype=jnp.float32) / half)
    ang = pos.astype(jnp.float32)[:, None] * inv_freq[None, :]
    cos = jnp.cos(ang)[:, None, :]
    sin = jnp.sin(ang)[:, None, :]
    x1 = x[..., :half].astype(jnp.float32)
    x2 = x[..., half:].astype(jnp.float32)
    return jnp.concatenate([x1 * cos - x2 * sin, x2 * cos + x1 * sin], axis=-1).astype(x.dtype)


def _causal(t):
    return jnp.tril(jnp.ones((t, t), dtype=bool))


def _project(x, pos, norm1, w_in, q_norm, kv_norm, w_uq, b_f):
    B, T, _ = x.shape
    z = _rmsnorm(x, norm1) @ w_in
    idx = [int(v) for v in np.cumsum(SPLITS)[:-1]]
    cq, ckv, kr, fq, fk, fv, fl, gate_a, gate_b, mg_a, mg_b = jnp.split(z, idx, axis=-1)
    q = (_rmsnorm(cq, q_norm) @ w_uq).reshape(B, T, MLA_HEADS, MLA_NOPE + MLA_ROPE)
    q_nope = q[..., :MLA_NOPE]
    q_rope = _rope(q[..., MLA_NOPE:], pos)
    ckv = _rmsnorm(ckv, kv_norm)
    kr = _rope(kr[:, :, None, :], pos)[:, :, 0, :]
    fq = fq.reshape(B, T, FOX_HEADS, FOX_HD)
    fk = fk.reshape(B, T, FOX_HEADS, FOX_HD)
    fv = fv.reshape(B, T, FOX_HEADS, FOX_HD)
    logf = jax.nn.log_sigmoid((fl + b_f).astype(jnp.float32))
    return q_nope, q_rope, ckv, kr, fq, fk, fv, logf, gate_a, gate_b, mg_a, mg_b


def _merge(x, p, o_a, o_b, gate_a, gate_b, mg_a, mg_b, w_up_a, w_up_b, w_o, w_ple, w_ple_gate):
    B, T, _ = x.shape
    y_a = (o_a.reshape(B, T, MLA_WIDTH) * jax.nn.silu(gate_a)) @ w_up_a
    y_b = (o_b.reshape(B, T, FOX_WIDTH) * jax.nn.silu(gate_b)) @ w_up_b
    mixed = jax.nn.sigmoid(mg_a) * y_a + jax.nn.sigmoid(mg_b) * y_b
    h = x + mixed @ w_o
    return h + jax.nn.sigmoid(h @ w_ple_gate) * (p @ w_ple)


def _mla_prompt(q_nope, q_rope, ckv, kr, w_uk, w_uv):
    S = q_nope.shape[1]
    k_nope = jnp.einsum('bsc,chd->bshd', ckv, w_uk)
    v = jnp.einsum('bsc,chd->bshd', ckv, w_uv)
    outs = []
    for s0 in range(0, S, Q_BLOCK):
        e = min(s0 + Q_BLOCK, S)
        s = (jnp.einsum('bthd,bshd->bhts', q_nope[:, s0:e], k_nope[:, :e])
             + jnp.einsum('bthr,bsr->bhts', q_rope[:, s0:e], kr[:, :e])).astype(jnp.float32) * MLA_SCALE
        mask = jnp.arange(e)[None, :] <= jnp.arange(s0, e)[:, None]
        pr = jax.nn.softmax(jnp.where(mask, s, NEG), axis=-1).astype(v.dtype)
        outs.append(jnp.einsum('bhts,bshd->bthd', pr, v[:, :e]))
    return jnp.concatenate(outs, axis=1)


def _fox_prompt(q, k, v, logf):
    S = q.shape[1]
    c = jnp.transpose(jnp.cumsum(logf, axis=1), (0, 2, 1))
    outs = []
    for s0 in range(0, S, Q_BLOCK):
        e = min(s0 + Q_BLOCK, S)
        s = jnp.einsum('bthd,bshd->bhts', q[:, s0:e], k[:, :e]).astype(jnp.float32) * FOX_SCALE
        s = s + c[:, :, s0:e, None] - c[:, :, None, :e]
        mask = jnp.arange(e)[None, :] <= jnp.arange(s0, e)[:, None]
        pr = jax.nn.softmax(jnp.where(mask, s, NEG), axis=-1).astype(v.dtype)
        outs.append(jnp.einsum('bhts,bshd->bthd', pr, v[:, :e]))
    return jnp.concatenate(outs, axis=1)


def _online_init(B, H, T, dv):
    return (jnp.full((B, H, T), NEG, jnp.float32), jnp.zeros((B, H, T), jnp.float32),
            jnp.zeros((B, H, T, dv), jnp.float32))


def _online(carry, s, pv):
    m, l, acc = carry
    m_new = jnp.maximum(m, jnp.max(s, axis=-1))
    a = jnp.exp(m - m_new)
    p = jnp.exp(s - m_new[..., None])
    return (m_new, l * a + jnp.sum(p, axis=-1), acc * a[..., None] + pv(p))


def _mla_sample(q_nope, q_rope, ckv_new, kr_new, cache_ckv, cache_kr, page_table, layer, w_uk, w_uv):
    B, T, H, _ = q_nope.shape
    q_lat = jnp.einsum('bthd,chd->bhtc', q_nope, w_uk)
    q_r = jnp.transpose(q_rope, (0, 2, 1, 3))

    def scores(ck, krp):
        s = jnp.einsum('bhtc,bpc->bhtp', q_lat, ck) + jnp.einsum('bhtr,bpr->bhtp', q_r, krp)
        return s.astype(jnp.float32) * MLA_SCALE

    def step(carry, phys):
        ck = cache_ckv[layer, phys]
        krp = cache_kr[layer, phys]
        return _online(carry, scores(ck, krp), lambda p: jnp.einsum('bhtp,bpc->bhtc', p, ck)), None

    carry, _ = lax.scan(step, _online_init(B, H, T, MLA_KV_RANK), page_table.T)
    s_new = jnp.where(_causal(T), scores(ckv_new, kr_new), NEG)
    _, l, acc = _online(carry, s_new, lambda p: jnp.einsum('bhtp,bpc->bhtc', p, ckv_new))
    o_lat = (acc / l[..., None]).astype(q_nope.dtype)
    return jnp.einsum('bhtc,chd->bthd', o_lat, w_uv)


def _fox_sample(q, k_new, v_new, logf_new, cache_k, cache_v, cache_logf, page_table, layer):
    B, T, H, D = q.shape
    qh = jnp.transpose(q, (0, 2, 1, 3))
    c_new = jnp.transpose(jnp.cumsum(logf_new, axis=1), (0, 2, 1))

    def step(carry, phys):
        st, after = carry
        kp = cache_k[layer, phys]
        vp = cache_v[layer, phys]
        lf = cache_logf[layer, phys].astype(jnp.float32)
        within = lax.cumsum(lf, axis=1, reverse=True) - lf
        suffix = jnp.transpose(within + after[:, None, :], (0, 2, 1))
        s = (jnp.einsum('bhtd,bphd->bhtp', qh, kp).astype(jnp.float32) * FOX_SCALE
             + c_new[..., :, None] + suffix[..., None, :])
        st = _online(st, s, lambda p: jnp.einsum('bhtp,bphd->bhtd', p, vp))
        return (st, after + jnp.sum(lf, axis=1)), None

    carry0 = (_online_init(B, H, T, D), jnp.zeros((B, H), jnp.float32))
    (st, _), _ = lax.scan(step, carry0, page_table.T[::-1])
    s_new = (jnp.einsum('bhtd,bshd->bhts', qh, k_new).astype(jnp.float32) * FOX_SCALE
             + c_new[..., :, None] - c_new[..., None, :])
    s_new = jnp.where(_causal(T), s_new, NEG)
    _, l, acc = _online(st, s_new, lambda p: jnp.einsum('bhts,bshd->bhtd', p, v_new))
    return jnp.transpose(acc / l[..., None], (0, 2, 1, 3)).astype(q.dtype)


def setup_inputs(seed: int = 0) -> dict:
    key = jax.random.key(seed)
    ks = jax.random.split(key, 24)
    f32 = jnp.float32
    n_pages = PAST_LEN // PAGE_SIZE
    n_used = DEC_BATCH * n_pages
    n_pool = n_used + n_used // 4

    def nrm(k, shape, scale=1.0):
        return jax.random.normal(k, shape, f32) * scale

    page_table = jax.random.permutation(ks[7], n_pool)[:n_used].reshape(DEC_BATCH, n_pages).astype(jnp.int32)
    return {
        "x_prompt": nrm(ks[0], (BATCH, SEQ, D_MODEL)),
        "x_sample": nrm(ks[1], (DEC_BATCH, DEC_SEQ, D_MODEL)),
        "cache_mla_ckv": nrm(ks[2], (DEPTH, n_pool, PAGE_SIZE, MLA_KV_RANK)),
        "cache_mla_krope": nrm(ks[3], (DEPTH, n_pool, PAGE_SIZE, MLA_ROPE)),
        "cache_fox_k": nrm(ks[4], (DEPTH, n_pool, PAGE_SIZE, FOX_HEADS, FOX_HD)),
        "cache_fox_v": nrm(ks[5], (DEPTH, n_pool, PAGE_SIZE, FOX_HEADS, FOX_HD)),
        "cache_fox_logf": jax.nn.log_sigmoid(2.0 + nrm(ks[6], (DEPTH, n_pool, PAGE_SIZE, FOX_HEADS))),
        "page_table": page_table,
        "p_prompt": nrm(ks[8], (DEPTH, BATCH, SEQ, PLE_DIM)),
        "p_sample": nrm(ks[9], (DEPTH, DEC_BATCH, DEC_SEQ, PLE_DIM)),
        "norm1": 1.0 + nrm(ks[10], (DEPTH, D_MODEL), 0.02),
        "w_in": nrm(ks[11], (DEPTH, D_MODEL, IN_WIDTH), D_MODEL ** -0.5),
        "q_norm": 1.0 + nrm(ks[12], (DEPTH, MLA_Q_RANK), 0.02),
        "kv_norm": 1.0 + nrm(ks[13], (DEPTH, MLA_KV_RANK), 0.02),
        "w_uq": nrm(ks[14], (DEPTH, MLA_Q_RANK, MLA_HEADS * (MLA_NOPE + MLA_ROPE)), MLA_Q_RANK ** -0.5),
        "w_uk": nrm(ks[15], (DEPTH, MLA_KV_RANK, MLA_HEADS, MLA_NOPE), MLA_KV_RANK ** -0.5),
        "w_uv": nrm(ks[16], (DEPTH, MLA_KV_RANK, MLA_HEADS, MLA_V), MLA_KV_RANK ** -0.5),
        "fox_forget_bias": 2.0 + nrm(ks[17], (DEPTH, FOX_HEADS), 0.5),
        "w_up_a": nrm(ks[18], (DEPTH, MLA_WIDTH, D_MODEL), MLA_WIDTH ** -0.5),
        "w_up_b": nrm(ks[19], (DEPTH, FOX_WIDTH, D_MODEL), FOX_WIDTH ** -0.5),
        "w_o": nrm(ks[20], (DEPTH, D_MODEL, D_MODEL), D_MODEL ** -0.5),
        "w_ple": nrm(ks[21], (DEPTH, PLE_DIM, D_MODEL), PLE_DIM ** -0.5),
        "w_ple_gate": nrm(ks[22], (DEPTH, D_MODEL, D_MODEL), D_MODEL ** -0.5),
        "final_norm": 1.0 + nrm(ks[23], (D_MODEL,), 0.02),
    }


def reference(x_prompt, x_sample, cache_mla_ckv, cache_mla_krope, cache_fox_k, cache_fox_v,
              cache_fox_logf, page_table, p_prompt, p_sample, norm1, w_in, q_norm, kv_norm, w_uq,
              w_uk, w_uv, fox_forget_bias, w_up_a, w_up_b, w_o, w_ple, w_ple_gate, final_norm):
    S = x_prompt.shape[1]
    T = x_sample.shape[1]
    past_len = page_table.shape[1] * cache_mla_ckv.shape[2]
    pos_p = jnp.arange(S, dtype=jnp.int32)
    pos_s = past_len + jnp.arange(T, dtype=jnp.int32)

    hp, hs = x_prompt, x_sample
    ckv_p, kr_p, fk_p, fv_p, lf_p = [], [], [], [], []
    ckv_s, kr_s, fk_s, fv_s, lf_s = [], [], [], [], []
    for i in range(DEPTH):
        qn, qr, ckv, kr, fq, fk, fv, lf, ga, gb, ma, mb = _project(
            hp, pos_p, norm1[i], w_in[i], q_norm[i], kv_norm[i], w_uq[i], fox_forget_bias[i])
        o_a = _mla_prompt(qn, qr, ckv, kr, w_uk[i], w_uv[i])
        o_b = _fox_prompt(fq, fk, fv, lf)
        hp = _merge(hp, p_prompt[i], o_a, o_b, ga, gb, ma, mb, w_up_a[i], w_up_b[i], w_o[i], w_ple[i], w_ple_gate[i])
        ckv_p.append(ckv); kr_p.append(kr); fk_p.append(fk); fv_p.append(fv); lf_p.append(lf)

        qn, qr, ckv, kr, fq, fk, fv, lf, ga, gb, ma, mb = _project(
            hs, pos_s, norm1[i], w_in[i], q_norm[i], kv_norm[i], w_uq[i], fox_forget_bias[i])
        o_a = _mla_sample(qn, qr, ckv, kr, cache_mla_ckv, cache_mla_krope, page_table, i, w_uk[i], w_uv[i])
        o_b = _fox_sample(fq, fk, fv, lf, cache_fox_k, cache_fox_v, cache_fox_logf, page_table, i)
        hs = _merge(hs, p_sample[i], o_a, o_b, ga, gb, ma, mb, w_up_a[i], w_up_b[i], w_o[i], w_ple[i], w_ple_gate[i])
        ckv_s.append(ckv); kr_s.append(kr); fk_s.append(fk); fv_s.append(fv); lf_s.append(lf)

    y_prompt = _rmsnorm(hp, final_norm)
    y_sample = _rmsnorm(hs, final_norm)
    return (y_prompt, y_sample,
            jnp.stack(ckv_p), jnp.stack(kr_p), jnp.stack(fk_p), jnp.stack(fv_p), jnp.stack(lf_p),
            jnp.stack(ckv_s), jnp.stack(kr_s), jnp.stack(fk_s), jnp.stack(fv_s), jnp.stack(lf_s))
```

```python
import functools
import math

import jax
import jax.numpy as jnp
import numpy as np
from jax import lax
from jax.experimental import pallas as pl
from jax.experimental.pallas import tpu as pltpu

D_MODEL = 1024
MLA_HEADS = 8
MLA_NOPE = 64
MLA_ROPE = 32
MLA_V = 64
MLA_Q_RANK = 256
MLA_KV_RANK = 128
MLA_SCALE = (MLA_NOPE + MLA_ROPE) ** -0.5
FOX_HEADS = 8
FOX_HD = 64
FOX_WIDTH = FOX_HEADS * FOX_HD
FOX_SCALE = FOX_HD ** -0.5
PLE_DIM = 256
ROPE_THETA = 10000.0
EPS = 1e-6
NEG = -1e30
HALF = MLA_ROPE // 2

LANES = 128
SUBLANES = 8
HEAD_BLOCK = LANES
VMEM_LIMIT = 56 * 1024 * 1024

A_CQ, A_CKV, A_KRA, A_KRB, A_FL, A_FQ, A_FK, A_FV, A_END = 0, 256, 384, 512, 640, 768, 1280, 1792, 2304

F32 = jnp.float32
BF16 = jnp.bfloat16


def _dot(a, b):
    return jnp.dot(a, b, preferred_element_type=F32)


def _dot_nt(a, b):
    return lax.dot_general(a, b, (((1,), (1,)), ((), ())), preferred_element_type=F32)


def _split3(x):
    hi = x.astype(BF16)
    r = x - hi.astype(F32)
    mid = r.astype(BF16)
    lo = (r - mid.astype(F32)).astype(BF16)
    return hi, mid, lo


def _dot3_rhs(a_bf16, x):
    hi, mid, lo = _split3(x)
    return _dot(a_bf16, lo) + _dot(a_bf16, mid) + _dot(a_bf16, hi)


def _dot3_lhs(x, b_bf16):
    hi, mid, lo = _split3(x)
    return _dot(lo, b_bf16) + _dot(mid, b_bf16) + _dot(hi, b_bf16)


def _rms(x, g):
    inv = lax.rsqrt(jnp.mean(x * x, axis=-1, keepdims=True) + EPS)
    return (x * inv) * g


def _sigmoid(x):
    return 1.0 / (1.0 + jnp.exp(-x))


def _log_sigmoid(x):
    return -(jnp.maximum(-x, 0.0) + jnp.log(1.0 + jnp.exp(-jnp.abs(x))))


def _rope_tables(pos_f32):
    rows = pos_f32.shape[0]
    lane = lax.broadcasted_iota(jnp.int32, (rows, LANES), 1)
    idx = (lane % HALF).astype(F32)
    inv_freq = jnp.exp(-math.log(ROPE_THETA) * idx / HALF)
    ang = pos_f32 * inv_freq
    cos = jnp.cos(ang)
    sin = jnp.sin(ang)
    cos_t = jnp.where(lane < MLA_ROPE, cos, jnp.where(lane < MLA_ROPE + MLA_NOPE, 1.0, 0.0))
    sin_t = jnp.where(lane < HALF, -sin, jnp.where(lane < MLA_ROPE, sin, 0.0))
    return cos_t, sin_t


def _project_common(x, n1, wa, qn, kvn, wq1, wq2, bf, cos_t, sin_t):
    xn = _rms(x, n1).astype(BF16)
    z = _dot(xn, wa)
    cqn = _rms(z[:, A_CQ:A_CKV], qn).astype(BF16)
    q1 = _dot(cqn, wq1)
    q2 = _dot(cqn, wq2)
    q_heads = []
    for h in range(MLA_HEADS):
        sl = slice(h * HEAD_BLOCK, (h + 1) * HEAD_BLOCK)
        q_heads.append(q1[:, sl] * cos_t + q2[:, sl] * sin_t)
    ckv = _rms(z[:, A_CKV:A_KRA], kvn)
    kr = z[:, A_KRA:A_KRB] * cos_t + z[:, A_KRB:A_FL] * sin_t
    logf = _log_sigmoid(z[:, A_FL:A_FQ] + bf)
    fq = z[:, A_FQ:A_FK] * FOX_SCALE
    fk = z[:, A_FK:A_FV]
    fv = z[:, A_FV:A_END]
    return q_heads, ckv, kr, fq, fk, fv, logf


def _proj_prompt_body(x_ref, n1_ref, wa_ref, qn_ref, kvn_ref, wq1_ref, wq2_ref, wk_ref, wv_ref, bf_ref,
                      qfull_ref, kfull_ref, vmla_ref, ckv_ref, kr_ref, fq_ref, fkb_ref, fvb_ref,
                      fk_ref, fv_ref, logf_ref, ct_ref, carry_ref, *, tm):
    s_idx = pl.program_id(1)
    pos = (s_idx * tm + lax.broadcasted_iota(jnp.int32, (tm, 1), 0)).astype(F32)
    cos_t, sin_t = _rope_tables(pos)
    q_heads, ckv, kr, fq, fk, fv, logf = _project_common(
        x_ref[...], n1_ref[...], wa_ref[...], qn_ref[...], kvn_ref[...], wq1_ref[...], wq2_ref[...],
        bf_ref[...], cos_t, sin_t)
    for h in range(MLA_HEADS):
        qfull_ref[:, h * HEAD_BLOCK:(h + 1) * HEAD_BLOCK] = q_heads[h].astype(BF16)
    ckv_ref[...] = ckv
    kr_ref[...] = kr[:, :MLA_ROPE]
    ckv_b = ckv.astype(BF16)
    kn = _dot(ckv_b, wk_ref[...])
    for h in range(MLA_HEADS):
        sl = slice(h * HEAD_BLOCK, (h + 1) * HEAD_BLOCK)
        kfull_ref[:, sl] = (kn[:, sl] + kr).astype(BF16)
    vmla_ref[...] = _dot(ckv_b, wv_ref[...]).astype(BF16)
    fq_ref[...] = fq.astype(BF16)
    fk_ref[...] = fk
    fv_ref[...] = fv
    fkb_ref[...] = fk.astype(BF16)
    fvb_ref[...] = fv.astype(BF16)
    logf_ref[...] = logf[:, :FOX_HEADS]

    @pl.when(s_idx == 0)
    def _():
        carry_ref[...] = jnp.zeros_like(carry_ref)

    row = lax.broadcasted_iota(jnp.int32, (tm, tm), 0)
    col = lax.broadcasted_iota(jnp.int32, (tm, tm), 1)
    tri = jnp.where(row >= col, 1.0, 0.0).astype(BF16)
    cum = _dot3_rhs(tri, logf) + carry_ref[0:1, :]
    carry_ref[0:1, :] = cum[tm - 1:tm, :]
    ct_ref[0] = cum.T[:FOX_HEADS, :]


def _proj_prompt(x2, n1, wa, qn, kvn, wq1, wq2, wk, wv, bf, *, batch, seq, tm):
    n = batch * seq
    ns = seq // tm
    row_map = lambda b, s: (b * ns + s, 0)
    const = lambda b, s: (0, 0)

    def rows(w):
        return pl.BlockSpec((tm, w), row_map)

    def full(a):
        return pl.BlockSpec(a.shape, const)

    out_shape = (
        jax.ShapeDtypeStruct((n, MLA_HEADS * HEAD_BLOCK), BF16),
        jax.ShapeDtypeStruct((n, MLA_HEADS * HEAD_BLOCK), BF16),
        jax.ShapeDtypeStruct((n, MLA_HEADS * MLA_V), BF16),
        jax.ShapeDtypeStruct((n, MLA_KV_RANK), F32),
        jax.ShapeDtypeStruct((n, MLA_ROPE), F32),
        jax.ShapeDtypeStruct((n, FOX_WIDTH), BF16),
        jax.ShapeDtypeStruct((n, FOX_WIDTH), BF16),
        jax.ShapeDtypeStruct((n, FOX_WIDTH), BF16),
        jax.ShapeDtypeStruct((n, FOX_WIDTH), F32),
        jax.ShapeDtypeStruct((n, FOX_WIDTH), F32),
        jax.ShapeDtypeStruct((n, FOX_HEADS), F32),
        jax.ShapeDtypeStruct((batch, FOX_HEADS, seq), F32),
    )
    out_specs = (
        rows(MLA_HEADS * HEAD_BLOCK), rows(MLA_HEADS * HEAD_BLOCK), rows(MLA_HEADS * MLA_V),
        rows(MLA_KV_RANK), rows(MLA_ROPE), rows(FOX_WIDTH), rows(FOX_WIDTH), rows(FOX_WIDTH),
        rows(FOX_WIDTH), rows(FOX_WIDTH), rows(FOX_HEADS),
        pl.BlockSpec((1, FOX_HEADS, tm), lambda b, s: (b, 0, s)),
    )
    return pl.pallas_call(
        functools.partial(_proj_prompt_body, tm=tm),
        grid=(batch, ns),
        in_specs=[rows(D_MODEL), full(n1), full(wa), full(qn), full(kvn), full(wq1), full(wq2), full(wk),
                  full(wv), full(bf)],
        out_specs=out_specs,
        out_shape=out_shape,
        scratch_shapes=[pltpu.VMEM((SUBLANES, LANES), F32)],
        compiler_params=pltpu.CompilerParams(dimension_semantics=("arbitrary", "arbitrary"),
                                             vmem_limit_bytes=VMEM_LIMIT),
        name="proj_prompt",
    )(x2, n1, wa, qn, kvn, wq1, wq2, wk, wv, bf)


def _flash_body(*refs, tq, scale, fox):
    if fox:
        q_ref, k_ref, v_ref, ct_ref, o_ref = refs
    else:
        q_ref, k_ref, v_ref, o_ref = refs
    qi = pl.program_id(2)
    tk = tq
    lane = lax.broadcasted_iota(jnp.int32, (tq, LANES), 1)
    low = lane < FOX_HD
    outs = []
    for j in range(2):
        if fox:
            q = q_ref[...]
            q = jnp.where(low if j == 0 else jnp.logical_not(low), q, jnp.zeros_like(q))
            ksl = slice(0, LANES)
        else:
            q = q_ref[:, j * HEAD_BLOCK:(j + 1) * HEAD_BLOCK]
            ksl = slice(j * HEAD_BLOCK, (j + 1) * HEAD_BLOCK)

        def step(kt, carry, masked, q=q, ksl=ksl, j=j):
            m, l, acc = carry
            start = pl.multiple_of(kt * tk, tk)
            k = k_ref[pl.ds(start, tk), ksl]
            s = _dot_nt(q, k)
            if scale != 1.0:
                s = s * scale
            if fox:
                s = s - ct_ref[0, 0, j:j + 1, pl.ds(start, tk)]
            if masked:
                r = lax.broadcasted_iota(jnp.int32, (tq, tk), 0)
                c = lax.broadcasted_iota(jnp.int32, (tq, tk), 1)
                s = jnp.where(r >= c, s, NEG)
            m_new = jnp.maximum(m, jnp.max(s, axis=1, keepdims=True))
            alpha = jnp.exp(m - m_new)
            p = jnp.exp(s - m_new)
            l = l * alpha + jnp.sum(p, axis=1, keepdims=True)
            acc = acc * alpha + _dot(p.astype(BF16), v_ref[pl.ds(start, tk), :])
            return m_new, l, acc

        init = (jnp.full((tq, 1), NEG, F32), jnp.zeros((tq, 1), F32), jnp.zeros((tq, LANES), F32))
        carry = lax.fori_loop(0, qi, functools.partial(step, masked=False), init)
        m, l, acc = step(qi, carry, True)
        outs.append(acc / l)
    o_ref[...] = jnp.where(low, outs[0], outs[1])


def _flash(q, k, v, ct, *, batch, seq, tq, scale, fox):
    n = batch * seq
    nq = seq // tq
    pairs = MLA_HEADS // 2
    qw = LANES if fox else 2 * HEAD_BLOCK
    in_specs = [
        pl.BlockSpec((tq, qw), lambda b, hp, i: (b * nq + i, hp)),
        pl.BlockSpec((seq, qw), lambda b, hp, i: (b, hp)),
        pl.BlockSpec((seq, LANES), lambda b, hp, i: (b, hp)),
    ]
    args = [q, k, v]
    if fox:
        in_specs.append(pl.BlockSpec((1, 1, 2, seq), lambda b, hp, i: (b, hp, 0, 0)))
        args.append(ct)
    return pl.pallas_call(
        functools.partial(_flash_body, tq=tq, scale=scale, fox=fox),
        grid=(batch, pairs, nq),
        in_specs=in_specs,
        out_specs=pl.BlockSpec((tq, LANES), lambda b, hp, i: (b * nq + i, hp)),
        out_shape=jax.ShapeDtypeStruct((n, FOX_WIDTH), F32),
        compiler_params=pltpu.CompilerParams(dimension_semantics=("arbitrary", "arbitrary", "arbitrary"),
                                             vmem_limit_bytes=VMEM_LIMIT),
        name="fox_prompt_attention" if fox else "mla_prompt_attention",
    )(*args)


def _merge_body(x_ref, p_ref, oa_ref, ob_ref, n1_ref, wg_ref, wua_ref, wub_ref, wo_ref, wple_ref, wpg_ref,
                fn_ref, y_ref):
    x = x_ref[...]
    xn = _rms(x, n1_ref[...]).astype(BF16)
    g = _dot(xn, wg_ref[...])
    ga = g[:, 0:512]
    gb = g[:, 512:1024]
    ma = g[:, 1024:2048]
    mb = g[:, 2048:3072]
    ua = (oa_ref[...] * (ga * _sigmoid(ga))).astype(BF16)
    ub = (ob_ref[...] * (gb * _sigmoid(gb))).astype(BF16)
    ya = _dot(ua, wua_ref[...])
    yb = _dot(ub, wub_ref[...])
    mixed = _sigmoid(ma) * ya + _sigmoid(mb) * yb
    h = x + _dot(mixed.astype(BF16), wo_ref[...])
    gate = _sigmoid(_dot(h.astype(BF16), wpg_ref[...]))
    h = h + gate * _dot(p_ref[...].astype(BF16), wple_ref[...])
    y_ref[...] = _rms(h, fn_ref[...])


def _merge(x2, p2, oa, ob, n1, wg, wua, wub, wo, wple, wpg, fn, *, tm):
    n = x2.shape[0]
    rows = lambda w: pl.BlockSpec((tm, w), lambda i: (i, 0))
    full = lambda a: pl.BlockSpec(a.shape, lambda i: (0, 0))
    return pl.pallas_call(
        _merge_body,
        grid=(n // tm,),
        in_specs=[rows(D_MODEL), rows(PLE_DIM), rows(512), rows(512), full(n1), full(wg), full(wua), full(wub),
                  full(wo), full(wple), full(wpg), full(fn)],
        out_specs=rows(D_MODEL),
        out_shape=jax.ShapeDtypeStruct((n, D_MODEL), F32),
        compiler_params=pltpu.CompilerParams(dimension_semantics=("arbitrary",), vmem_limit_bytes=VMEM_LIMIT),
        name="merge",
    )(x2, p2, oa, ob, n1, wg, wua, wub, wo, wple, wpg, fn)


def _proj_sample_body(x_ref, n1_ref, wa_ref, qn_ref, kvn_ref, wq1_ref, wq2_ref, wk_ref, bf_ref,
                      ckv_ref, kr_ref, fq_ref, fk_ref, fv_ref, logf_ref, qcat_ref, *, past_len):
    rows = x_ref.shape[0]
    pos = jnp.full((rows, 1), past_len, jnp.int32).astype(F32)
    cos_t, sin_t = _rope_tables(pos)
    q_heads, ckv, kr, fq, fk, fv, logf = _project_common(
        x_ref[...], n1_ref[...], wa_ref[...], qn_ref[...], kvn_ref[...], wq1_ref[...], wq2_ref[...],
        bf_ref[...], cos_t, sin_t)
    ckv_ref[...] = ckv
    kr_ref[...] = kr[:, :MLA_ROPE]
    fq_ref[...] = fq
    fk_ref[...] = fk
    fv_ref[...] = fv
    logf_ref[...] = logf
    lane = lax.broadcasted_iota(jnp.int32, (rows, LANES), 1)
    for h in range(MLA_HEADS):
        qh = q_heads[h]
        q_lat = _dot_nt(qh.astype(BF16), wk_ref[:, h * HEAD_BLOCK:(h + 1) * HEAD_BLOCK])
        qcat_ref[h, :, 0:LANES] = q_lat
        qcat_ref[h, :, LANES:2 * LANES] = jnp.where(lane < MLA_ROPE, qh, 0.0)


def _proj_sample(xs, n1, wa, qn, kvn, wq1, wq2, wk, bf, *, past_len):
    nb = xs.shape[0]
    full = lambda a: pl.BlockSpec(a.shape, lambda i: (0,) * a.ndim)
    out_shape = (
        jax.ShapeDtypeStruct((nb, MLA_KV_RANK), F32),
        jax.ShapeDtypeStruct((nb, MLA_ROPE), F32),
        jax.ShapeDtypeStruct((nb, FOX_WIDTH), F32),
        jax.ShapeDtypeStruct((nb, FOX_WIDTH), F32),
        jax.ShapeDtypeStruct((nb, FOX_WIDTH), F32),
        jax.ShapeDtypeStruct((nb, LANES), F32),
        jax.ShapeDtypeStruct((MLA_HEADS, nb, 2 * LANES), F32),
    )
    out_specs = tuple(pl.BlockSpec(s.shape, functools.partial(lambda nd, i: (0,) * nd, len(s.shape)))
                      for s in out_shape)
    return pl.pallas_call(
        functools.partial(_proj_sample_body, past_len=past_len),
        grid=(1,),
        in_specs=[full(xs), full(n1), full(wa), full(qn), full(kvn), full(wq1), full(wq2), full(wk), full(bf)],
        out_specs=out_specs,
        out_shape=out_shape,
        compiler_params=pltpu.CompilerParams(dimension_semantics=("arbitrary",), vmem_limit_bytes=VMEM_LIMIT),
        name="proj_sample",
    )(xs, n1, wa, qn, kvn, wq1, wq2, wk, bf)


PAGES_PER_STEP = 8


def _suffix_constants(page):
    p = np.arange(page)
    later = (p[:, None] > p[None, :]).astype(np.float32)
    return jnp.asarray(np.concatenate([later, np.ones((page, page), np.float32)], axis=1), BF16)


def _decode_body(pt_ref, fq_ref, qcat_ref, knew_ref, vnew_ref, lfnew_ref, ckvnew_ref, krnew_ref, wuv_ref,
                 uu_ref, kc_hbm, vc_hbm, ckv_hbm, kr_hbm, lf_hbm,
                 oa_ref, ob_ref,
                 kbuf, vbuf, cbuf, rbuf, lbuf, sems, mf, lf, accf, mm, lm, accm, aft, *, n_pages, page):
    g_pages = PAGES_PER_STEP
    nch = n_pages // g_pages
    t = pl.program_id(0)
    nt = pl.num_programs(0)
    c_idx = t % nch

    def copies(tt, slot):
        bb = tt // nch
        cc = nch - 1 - tt % nch
        out = []
        for g in range(g_pages):
            pg = pt_ref[bb * n_pages + cc * g_pages + g]
            lanes = pl.ds(g * page, page)
            out.append(pltpu.make_async_copy(kc_hbm.at[pg], kbuf.at[slot, :, lanes], sems.at[0, slot]))
            out.append(pltpu.make_async_copy(vc_hbm.at[pg], vbuf.at[slot, g], sems.at[1, slot]))
            out.append(pltpu.make_async_copy(ckv_hbm.at[pg], cbuf.at[slot, pl.ds(g * page, page), :],
                                             sems.at[2, slot]))
            out.append(pltpu.make_async_copy(kr_hbm.at[pg], rbuf.at[slot, :, lanes], sems.at[3, slot]))
            out.append(pltpu.make_async_copy(lf_hbm.at[pg], lbuf.at[slot, pl.ds(g * SUBLANES, SUBLANES), :],
                                             sems.at[4, slot]))
        return out

    slot = t % 2

    @pl.when(t == 0)
    def _():
        for cp in copies(t, slot):
            cp.start()

    @pl.when(t + 1 < nt)
    def _():
        for cp in copies(t + 1, 1 - slot):
            cp.start()

    for cp in copies(t, slot):
        cp.wait()

    row8 = lax.broadcasted_iota(jnp.int32, (FOX_HEADS, FOX_WIDTH), 0)
    col8 = lax.broadcasted_iota(jnp.int32, (FOX_HEADS, FOX_WIDTH), 1)
    diag = (col8 // FOX_HD) == row8
    qm = jnp.where(diag, jnp.broadcast_to(fq_ref[0], (FOX_HEADS, FOX_WIDTH)), 0.0)
    qcat = qcat_ref[0]
    q_lat = qcat[:, :MLA_KV_RANK]
    q_rope = qcat[:, MLA_KV_RANK:MLA_KV_RANK + MLA_ROPE]

    @pl.when(c_idx == 0)
    def _():
        mf[...] = jnp.full_like(mf, NEG)
        mm[...] = jnp.full_like(mm, NEG)
        lf[...] = jnp.zeros_like(lf)
        lm[...] = jnp.zeros_like(lm)
        accf[...] = jnp.zeros_like(accf)
        accm[...] = jnp.zeros_like(accm)
        aft[...] = jnp.zeros_like(aft)

    def softmax_step(m_ref, l_ref, s):
        m_old = m_ref[...]
        m_new = jnp.maximum(m_old, jnp.max(s, axis=1, keepdims=True))
        alpha = jnp.exp(m_old - m_new)
        p = jnp.exp(s - m_new[:, 0:1])
        l_ref[...] = l_ref[...] * alpha + jnp.sum(p, axis=1, keepdims=True)
        m_ref[...] = m_new
        return alpha, p

    z = _dot3_lhs(lbuf[slot], uu_ref[...])
    after = aft[...]
    biases = [None] * g_pages
    for g in range(g_pages - 1, -1, -1):
        rows = slice(g * SUBLANES, (g + 1) * SUBLANES)
        biases[g] = z[rows, :page] + after
        after = after + z[rows, page:]
    aft[...] = after
    s_f = _dot(qm, kbuf[slot]) + jnp.concatenate(biases, axis=1)
    alpha_f, p_f = softmax_step(mf, lf, s_f)
    for h in range(FOX_HEADS):
        rows = slice(h * FOX_HD, (h + 1) * FOX_HD)
        a = accf[rows, :] * alpha_f[h:h + 1, :]
        for g in range(g_pages):
            a = a + vbuf[slot, g, rows, :] * p_f[h:h + 1, g * page:(g + 1) * page]
        accf[rows, :] = a

    cpg = cbuf[slot]
    s_m = (_dot_nt(q_lat, cpg) + _dot(q_rope, rbuf[slot])) * MLA_SCALE
    alpha_m, p_m = softmax_step(mm, lm, s_m)
    accm[...] = accm[...] * alpha_m + _dot(p_m, cpg)

    @pl.when(c_idx == nch - 1)
    def _():
        def expand(col):
            wide = jnp.concatenate([col] * (FOX_WIDTH // LANES), axis=1)
            return jnp.sum(jnp.where(diag, wide, 0.0), axis=0, keepdims=True)

        s_new = jnp.sum(qm * knew_ref[0], axis=1, keepdims=True) - lfnew_ref[0]
        alpha, p_new = softmax_step(mf, lf, s_new[:, 0:1])
        past = _dot_nt(jnp.ones((SUBLANES, LANES), F32), accf[...])[0:1, :]
        ob_ref[0] = (past * expand(alpha) + expand(jnp.broadcast_to(p_new, (FOX_HEADS, LANES))) * vnew_ref[0]) \
            / expand(lf[...])
        s_new = (jnp.sum(q_lat * ckvnew_ref[0], axis=1, keepdims=True)
                 + jnp.sum(q_rope * krnew_ref[0], axis=1, keepdims=True)) * MLA_SCALE
        alpha, p_new = softmax_step(mm, lm, s_new)
        o_lat = (accm[...] * alpha + p_new * ckvnew_ref[0]) / lm[...]
        o_full = _dot(o_lat, wuv_ref[...])
        oa_ref[0] = jnp.sum(jnp.where(diag, o_full, 0.0), axis=0, keepdims=True)


def _decode(page_flat, fq3, qcat, knew3, vnew3, lfnew3, ckvnew3, krnew3, wuv, kc, vc, ckvc, krc, lfc,
            *, nb, n_pages, page):
    nch = n_pages // PAGES_PER_STEP
    uu = _suffix_constants(page)
    per_b = lambda shape: pl.BlockSpec((1,) + shape, lambda t, pt: (t // nch, 0, 0))
    const2 = lambda a: pl.BlockSpec(a.shape, lambda t, pt: (0, 0))
    anyspec = pl.BlockSpec(memory_space=pl.ANY)
    grid_spec = pltpu.PrefetchScalarGridSpec(
        num_scalar_prefetch=1,
        grid=(nb * nch,),
        in_specs=[per_b((1, FOX_WIDTH)), per_b((MLA_HEADS, 2 * LANES)), per_b((1, FOX_WIDTH)),
                  per_b((1, FOX_WIDTH)), per_b((FOX_HEADS, LANES)), per_b((1, MLA_KV_RANK)),
                  per_b((1, MLA_ROPE)), const2(wuv), const2(uu),
                  anyspec, anyspec, anyspec, anyspec, anyspec],
        out_specs=(per_b((1, FOX_WIDTH)), per_b((1, FOX_WIDTH))),
        scratch_shapes=[
            pltpu.VMEM((2, FOX_WIDTH, PAGES_PER_STEP * page), F32),
            pltpu.VMEM((2, PAGES_PER_STEP, FOX_WIDTH, page), F32),
            pltpu.VMEM((2, PAGES_PER_STEP * page, MLA_KV_RANK), F32),
            pltpu.VMEM((2, MLA_ROPE, PAGES_PER_STEP * page), F32),
            pltpu.VMEM((2, PAGES_PER_STEP * SUBLANES, page), F32),
            pltpu.SemaphoreType.DMA((5, 2)),
            pltpu.VMEM((FOX_HEADS, LANES), F32), pltpu.VMEM((FOX_HEADS, LANES), F32),
            pltpu.VMEM((FOX_WIDTH, page), F32),
            pltpu.VMEM((MLA_HEADS, LANES), F32), pltpu.VMEM((MLA_HEADS, LANES), F32),
            pltpu.VMEM((MLA_HEADS, MLA_KV_RANK), F32),
            pltpu.VMEM((FOX_HEADS, LANES), F32),
        ],
    )
    return pl.pallas_call(
        functools.partial(_decode_body, n_pages=n_pages, page=page),
        grid_spec=grid_spec,
        out_shape=(jax.ShapeDtypeStruct((nb, 1, FOX_WIDTH), F32), jax.ShapeDtypeStruct((nb, 1, FOX_WIDTH), F32)),
        compiler_params=pltpu.CompilerParams(dimension_semantics=("arbitrary",), vmem_limit_bytes=VMEM_LIMIT),
        name="paged_decode_attention",
    )(page_flat, fq3, qcat, knew3, vnew3, lfnew3, ckvnew3, krnew3, wuv, uu, kc, vc, ckvc, krc, lfc)


def _pack_weights(w_in, w_uq, w_uk, w_uv, fox_forget_bias):
    o = np.cumsum((0, MLA_Q_RANK, MLA_KV_RANK, MLA_ROPE, FOX_WIDTH, FOX_WIDTH, FOX_WIDTH, FOX_HEADS,
                   512, 512, D_MODEL, D_MODEL))
    cq, ckv, kr, fq, fk, fv, fl, ga, gb, ma, mb = [w_in[:, o[i]:o[i + 1]] for i in range(11)]
    pad = lambda w, width: jnp.pad(w, ((0, 0), (0, width - w.shape[1])))
    k1, k2 = kr[:, :HALF], kr[:, HALF:]
    wa = jnp.concatenate([cq, ckv, pad(jnp.concatenate([k1, k2], 1), LANES),
                          pad(jnp.concatenate([k2, k1], 1), LANES), pad(fl, LANES), fq, fk, fv], axis=1)
    wg = jnp.concatenate([ga, gb, ma, mb], axis=1)
    uq = w_uq.reshape(MLA_Q_RANK, MLA_HEADS, MLA_NOPE + MLA_ROPE)
    nope, x1, x2 = uq[..., :MLA_NOPE], uq[..., MLA_NOPE:MLA_NOPE + HALF], uq[..., MLA_NOPE + HALF:]
    zpad = jnp.zeros((MLA_Q_RANK, MLA_HEADS, HEAD_BLOCK - MLA_ROPE - MLA_NOPE), w_uq.dtype)
    wq1 = jnp.concatenate([x1, x2, nope, zpad], axis=-1).reshape(MLA_Q_RANK, MLA_HEADS * HEAD_BLOCK)
    wq2 = jnp.concatenate([x2, x1, jnp.zeros_like(nope), zpad], axis=-1).reshape(MLA_Q_RANK,
                                                                                   MLA_HEADS * HEAD_BLOCK)
    zk = jnp.zeros((MLA_KV_RANK, MLA_HEADS, MLA_ROPE), w_uk.dtype)
    wk = jnp.concatenate([zk, w_uk, zk], axis=-1).reshape(MLA_KV_RANK, MLA_HEADS * HEAD_BLOCK)
    wv = w_uv.reshape(MLA_KV_RANK, MLA_HEADS * MLA_V)
    bf = pad(fox_forget_bias.reshape(1, FOX_HEADS), LANES)
    return wa.astype(BF16), wg.astype(BF16), wq1.astype(BF16), wq2.astype(BF16), wk.astype(BF16), \
        wv.astype(BF16), bf


def kernel(x_prompt, x_sample, cache_mla_ckv, cache_mla_krope, cache_fox_k, cache_fox_v, cache_fox_logf,
           page_table, p_prompt, p_sample, norm1, w_in, q_norm, kv_norm, w_uq, w_uk, w_uv, fox_forget_bias,
           w_up_a, w_up_b, w_o, w_ple, w_ple_gate, final_norm):
    depth = norm1.shape[0]
    assert depth == 1, "single-layer step"
    batch, seq, _ = x_prompt.shape
    nb, dec_seq, _ = x_sample.shape
    assert dec_seq == 1, "one new token per decode sequence"
    n_pool, page = cache_mla_ckv.shape[1], cache_mla_ckv.shape[2]
    n_pages = page_table.shape[1]
    past_len = n_pages * page
    assert page == LANES and n_pages % PAGES_PER_STEP == 0

    wa, wg, wq1, wq2, wk, wv, bf = _pack_weights(w_in[0], w_uq[0], w_uk[0], w_uv[0], fox_forget_bias[0])
    n1 = norm1[0].reshape(1, D_MODEL)
    qn = q_norm[0].reshape(1, MLA_Q_RANK)
    kvn = kv_norm[0].reshape(1, MLA_KV_RANK)
    fn = final_norm.reshape(1, D_MODEL)
    wua, wub, wo = w_up_a[0].astype(BF16), w_up_b[0].astype(BF16), w_o[0].astype(BF16)
    wple, wpg = w_ple[0].astype(BF16), w_ple_gate[0].astype(BF16)

    n = batch * seq
    x2 = x_prompt.reshape(n, D_MODEL)
    (q_full, k_full, v_mla, ckv_p, kr_p, fq_b, fk_b, fv_b, fk_p, fv_p, lf_p, ct) = _proj_prompt(
        x2, n1, wa, qn, kvn, wq1, wq2, wk, wv, bf, batch=batch, seq=seq, tm=256)
    o_a = _flash(q_full, k_full, v_mla, None, batch=batch, seq=seq, tq=256, scale=MLA_SCALE, fox=False)
    ct4 = ct.reshape(batch, FOX_HEADS // 2, 2, seq)
    o_b = _flash(fq_b, fk_b, fv_b, ct4, batch=batch, seq=seq, tq=256, scale=1.0, fox=True)
    y_p = _merge(x2, p_prompt[0].reshape(n, PLE_DIM), o_a, o_b, n1, wg, wua, wub, wo, wple, wpg, fn, tm=256)

    xs = x_sample.reshape(nb, D_MODEL)
    ckv_s, kr_s, fq_s, fk_s, fv_s, lf_s, qcat = _proj_sample(xs, n1, wa, qn, kvn, wq1, wq2, wk, bf,
                                                              past_len=past_len)
    page_flat = page_table.reshape(nb * n_pages)
    lf_cols = jnp.broadcast_to(lf_s[:, :FOX_HEADS, None], (nb, FOX_HEADS, LANES))
    kc = jnp.transpose(cache_fox_k[0], (0, 2, 3, 1)).reshape(n_pool, FOX_WIDTH, page)
    vc = jnp.transpose(cache_fox_v[0], (0, 2, 3, 1)).reshape(n_pool, FOX_WIDTH, page)
    krc = jnp.transpose(cache_mla_krope[0], (0, 2, 1))
    lfc = jnp.transpose(cache_fox_logf[0], (0, 2, 1))
    oa_s, ob_s = _decode(
        page_flat, fq_s.reshape(nb, 1, FOX_WIDTH), jnp.transpose(qcat, (1, 0, 2)),
        fk_s.reshape(nb, 1, FOX_WIDTH), fv_s.reshape(nb, 1, FOX_WIDTH), lf_cols,
        ckv_s.reshape(nb, 1, MLA_KV_RANK), kr_s.reshape(nb, 1, MLA_ROPE), wv,
        kc, vc, cache_mla_ckv[0], krc, lfc, nb=nb, n_pages=n_pages, page=page)
    y_s = _merge(xs, p_sample[0].reshape(nb, PLE_DIM), oa_s.reshape(nb, 512), ob_s.reshape(nb, 512),
                 n1, wg, wua, wub, wo, wple, wpg, fn, tm=nb)

    return (y_p.reshape(batch, seq, D_MODEL), y_s.reshape(nb, 1, D_MODEL),
            ckv_p.reshape(1, batch, seq, MLA_KV_RANK), kr_p.reshape(1, batch, seq, MLA_ROPE),
            fk_p.reshape(1, batch, seq, FOX_HEADS, FOX_HD), fv_p.reshape(1, batch, seq, FOX_HEADS, FOX_HD),
            lf_p.reshape(1, batch, seq, FOX_HEADS),
            ckv_s.reshape(1, nb, 1, MLA_KV_RANK), kr_s.reshape(1, nb, 1, MLA_ROPE),
            fk_s.reshape(1, nb, 1, FOX_HEADS, FOX_HD), fv_s.reshape(1, nb, 1, FOX_HEADS, FOX_HD),
            lf_s[:, :FOX_HEADS].reshape(1, nb, 1, FOX_HEADS))
```

```python
import functools
import math

import jax
import jax.numpy as jnp
import numpy as np
from jax import lax
from jax.experimental import pallas as pl
from jax.experimental.pallas import tpu as pltpu

D_MODEL = 1024
MLA_HEADS = 8
MLA_NOPE = 64
MLA_ROPE = 32
MLA_V = 64
MLA_Q_RANK = 256
MLA_KV_RANK = 128
MLA_SCALE = (MLA_NOPE + MLA_ROPE) ** -0.5
FOX_HEADS = 8
FOX_HD = 64
FOX_WIDTH = FOX_HEADS * FOX_HD
FOX_SCALE = FOX_HD ** -0.5
PLE_DIM = 256
ROPE_THETA = 10000.0
EPS = 1e-6
NEG = -1e30
HALF = MLA_ROPE // 2
LOG2E = math.log2(math.e)

LANES = 128
SUBLANES = 8
HEAD_BLOCK = LANES
VMEM_LIMIT = 56 * 1024 * 1024

A_CQ, A_CKV, A_KRA, A_KRB, A_FL, A_FQ, A_FK, A_FV, A_END = 0, 256, 384, 512, 640, 768, 1280, 1792, 2304

F32 = jnp.float32
BF16 = jnp.bfloat16


def _dot(a, b):
    return jnp.dot(a, b, preferred_element_type=F32)


def _dot_nt(a, b):
    return lax.dot_general(a, b, (((1,), (1,)), ((), ())), preferred_element_type=F32)


def _split3(x):
    hi = x.astype(BF16)
    r = x - hi.astype(F32)
    mid = r.astype(BF16)
    lo = (r - mid.astype(F32)).astype(BF16)
    return hi, mid, lo


def _dot3_rhs(a_bf16, x):
    hi, mid, lo = _split3(x)
    return _dot(a_bf16, lo) + _dot(a_bf16, mid) + _dot(a_bf16, hi)


def _dot3_lhs(x, b_bf16):
    hi, mid, lo = _split3(x)
    return _dot(lo, b_bf16) + _dot(mid, b_bf16) + _dot(hi, b_bf16)


def _rms(x, g):
    inv = lax.rsqrt(jnp.mean(x * x, axis=-1, keepdims=True) + EPS)
    return (x * inv) * g


def _sigmoid(x):
    return 1.0 / (1.0 + jnp.exp(-x))


def _log_sigmoid(x):
    return -(jnp.maximum(-x, 0.0) + jnp.log(1.0 + jnp.exp(-jnp.abs(x))))


def _rope_tables(pos_f32):
    rows = pos_f32.shape[0]
    lane = lax.broadcasted_iota(jnp.int32, (rows, LANES), 1)
    idx = (lane % HALF).astype(F32)
    inv_freq = jnp.exp(-math.log(ROPE_THETA) * idx / HALF)
    ang = pos_f32 * inv_freq
    cos = jnp.cos(ang)
    sin = jnp.sin(ang)
    cos_t = jnp.where(lane < MLA_ROPE, cos, jnp.where(lane < MLA_ROPE + MLA_NOPE, 1.0, 0.0))
    sin_t = jnp.where(lane < HALF, -sin, jnp.where(lane < MLA_ROPE, sin, 0.0))
    return cos_t, sin_t


def _project_common(xn, wa, qn, kvn, wq1, wq2, bf, cos_t, sin_t):
    z = _dot(xn, wa)
    cqn = _rms(z[:, A_CQ:A_CKV], qn).astype(BF16)
    q1 = _dot(cqn, wq1)
    q2 = _dot(cqn, wq2)
    q_heads = []
    for h in range(MLA_HEADS):
        sl = slice(h * HEAD_BLOCK, (h + 1) * HEAD_BLOCK)
        q_heads.append(q1[:, sl] * cos_t + q2[:, sl] * sin_t)
    ckv = _rms(z[:, A_CKV:A_KRA], kvn)
    kr = z[:, A_KRA:A_KRB] * cos_t + z[:, A_KRB:A_FL] * sin_t
    logf = _log_sigmoid(z[:, A_FL:A_FQ] + bf)
    fq = z[:, A_FQ:A_FK] * FOX_SCALE
    fk = z[:, A_FK:A_FV]
    fv = z[:, A_FV:A_END]
    return q_heads, ckv, kr, fq, fk, fv, logf


def _proj_prompt_body(x_ref, n1_ref, wa_ref, qn_ref, kvn_ref, wq1_ref, wq2_ref, wk_ref, wvt_ref, bf_ref, wkvt_ref,
                      rep_ref,
                      qfull_ref, kfull_ref, vmlat_ref, ckv_ref, kr_ref, fq_ref, fkb_ref, fvtb_ref,
                      fkt_ref, fvt_ref, logf_ref, ccol_ref, carry_ref, *, tm):
    s_idx = pl.program_id(1)
    pos = (s_idx * tm + lax.broadcasted_iota(jnp.int32, (tm, 1), 0)).astype(F32)
    cos_t, sin_t = _rope_tables(pos)
    xn = _rms(x_ref[...], n1_ref[...]).astype(BF16)
    q_heads, ckv, kr, fq, fk, fv, logf = _project_common(
        xn, wa_ref[...], qn_ref[...], kvn_ref[...], wq1_ref[...], wq2_ref[...], bf_ref[...], cos_t, sin_t)
    for h in range(MLA_HEADS):
        qfull_ref[:, h * HEAD_BLOCK:(h + 1) * HEAD_BLOCK] = (q_heads[h] * (MLA_SCALE * LOG2E)).astype(BF16)
    ckv_ref[...] = ckv
    kr_ref[...] = kr[:, :MLA_ROPE]
    ckv_b = ckv.astype(BF16)
    kn = _dot(ckv_b, wk_ref[...])
    for h in range(MLA_HEADS):
        sl = slice(h * HEAD_BLOCK, (h + 1) * HEAD_BLOCK)
        kfull_ref[:, sl] = (kn[:, sl] + kr).astype(BF16)
    vmlat_ref[0] = _dot_nt(wvt_ref[...], ckv_b).astype(BF16)
    fq_ref[...] = (fq * LOG2E).astype(BF16)
    fkb_ref[...] = fk.astype(BF16)
    kvt = _dot_nt(wkvt_ref[...], xn)
    fkt_ref[0] = kvt[:FOX_WIDTH, :]
    fvt_ref[0] = kvt[FOX_WIDTH:, :]
    fvtb_ref[0] = kvt[FOX_WIDTH:, :].astype(BF16)
    logf_ref[...] = logf[:, :FOX_HEADS]

    @pl.when(s_idx == 0)
    def _():
        carry_ref[...] = jnp.zeros_like(carry_ref)

    row = lax.broadcasted_iota(jnp.int32, (tm, tm), 0)
    col = lax.broadcasted_iota(jnp.int32, (tm, tm), 1)
    tri = jnp.where(row >= col, 1.0, 0.0).astype(BF16)
    cum = _dot3_rhs(tri, logf) + carry_ref[0:1, :]
    carry_ref[0:1, :] = cum[tm - 1:tm, :]
    ccol_ref[...] = _dot3_lhs(cum * LOG2E, rep_ref[...])


def _pair_columns():
    rep = np.zeros((LANES, (FOX_HEADS // 2) * LANES), np.float32)
    for hp in range(FOX_HEADS // 2):
        for l in range(LANES):
            rep[2 * hp + l % 2, hp * LANES + l] = 1.0
    return jnp.asarray(rep, BF16)


def _proj_prompt(x2, n1, wa, qn, kvn, wq1, wq2, wk, wvt, bf, wkvt, *, batch, seq, tm):
    n = batch * seq
    ns = seq // tm
    row_map = lambda b, s: (b * ns + s, 0)
    const = lambda b, s: (0, 0)

    def rows(w):
        return pl.BlockSpec((tm, w), row_map)

    def full(a):
        return pl.BlockSpec(a.shape, const)

    rep = _pair_columns()
    out_shape = (
        jax.ShapeDtypeStruct((n, MLA_HEADS * HEAD_BLOCK), BF16),
        jax.ShapeDtypeStruct((n, MLA_HEADS * HEAD_BLOCK), BF16),
        jax.ShapeDtypeStruct((batch, MLA_HEADS * MLA_V, seq), BF16),
        jax.ShapeDtypeStruct((n, MLA_KV_RANK), F32),
        jax.ShapeDtypeStruct((n, MLA_ROPE), F32),
        jax.ShapeDtypeStruct((n, FOX_WIDTH), BF16),
        jax.ShapeDtypeStruct((n, FOX_WIDTH), BF16),
        jax.ShapeDtypeStruct((batch, FOX_WIDTH, seq), BF16),
        jax.ShapeDtypeStruct((batch, FOX_WIDTH, seq), F32),
        jax.ShapeDtypeStruct((batch, FOX_WIDTH, seq), F32),
        jax.ShapeDtypeStruct((n, FOX_HEADS), F32),
        jax.ShapeDtypeStruct((n, (FOX_HEADS // 2) * LANES), F32),
    )
    cols = lambda r: pl.BlockSpec((1, r, tm), lambda b, s: (b, 0, s))
    out_specs = (
        rows(MLA_HEADS * HEAD_BLOCK), rows(MLA_HEADS * HEAD_BLOCK), cols(MLA_HEADS * MLA_V),
        rows(MLA_KV_RANK), rows(MLA_ROPE), rows(FOX_WIDTH), rows(FOX_WIDTH), cols(FOX_WIDTH),
        cols(FOX_WIDTH), cols(FOX_WIDTH), rows(FOX_HEADS), rows((FOX_HEADS // 2) * LANES),
    )
    return pl.pallas_call(
        functools.partial(_proj_prompt_body, tm=tm),
        grid=(batch, ns),
        in_specs=[rows(D_MODEL), full(n1), full(wa), full(qn), full(kvn), full(wq1), full(wq2), full(wk),
                  full(wvt), full(bf), full(wkvt), full(rep)],
        out_specs=out_specs,
        out_shape=out_shape,
        scratch_shapes=[pltpu.VMEM((SUBLANES, LANES), F32)],
        compiler_params=pltpu.CompilerParams(dimension_semantics=("arbitrary", "arbitrary"),
                                             vmem_limit_bytes=VMEM_LIMIT),
        name="proj_prompt",
    )(x2, n1, wa, qn, kvn, wq1, wq2, wk, wvt, bf, wkvt, rep)


ONES_ROWS = 16


def _flash_body(*refs, tq, fox):
    if fox:
        q_ref, k_ref, vt_ref, c_ref, o_ref, m_sc, acc_sc = refs
    else:
        q_ref, k_ref, vt_ref, o_ref, m_sc, acc_sc = refs
    qi = pl.program_id(2)
    tk = tq
    if fox:
        qv = q_ref[...]
        zero = jnp.zeros_like(qv)
        low = lax.broadcasted_iota(jnp.int32, (tq, LANES), 1) < FOX_HD
        qs = (jnp.where(low, qv, zero), jnp.where(low, zero, qv))
        ksl = (slice(0, LANES), slice(0, LANES))
    else:
        qs = (q_ref[:, 0:HEAD_BLOCK], q_ref[:, HEAD_BLOCK:2 * HEAD_BLOCK])
        ksl = (slice(0, HEAD_BLOCK), slice(HEAD_BLOCK, 2 * HEAD_BLOCK))
    m_sc[...] = jnp.full_like(m_sc, NEG)
    acc_sc[...] = jnp.zeros_like(acc_sc)
    ones = jnp.ones((ONES_ROWS, tk), BF16)

    def tile(kt, masked):
        start = pl.multiple_of(kt * tk, tk)
        for j in range(2):
            st = _dot_nt(k_ref[pl.ds(start, tk), ksl[j]], qs[j])
            if fox:
                st = st - c_ref[pl.ds(start, tk), j:j + 1]
            if masked:
                r = lax.broadcasted_iota(jnp.int32, (tk, tq), 0)
                c = lax.broadcasted_iota(jnp.int32, (tk, tq), 1)
                st = jnp.where(r <= c, st, NEG)
            m_old = m_sc[j]
            m_new = jnp.maximum(m_old, jnp.max(st, axis=0, keepdims=True))
            alpha = jnp.exp2(m_old - m_new)
            pt = jnp.exp2(st - m_new).astype(BF16)
            vt = jnp.concatenate([vt_ref[0, j * MLA_V:(j + 1) * MLA_V, pl.ds(start, tk)], ones], axis=0)
            acc_sc[j] = acc_sc[j] * alpha + _dot(vt, pt)
            m_sc[j] = m_new

    def body(kt, carry):
        tile(kt, False)
        return carry

    lax.fori_loop(0, qi, body, 0)
    tile(qi, True)
    outs = [acc_sc[j, :MLA_V, :] / acc_sc[j, MLA_V:MLA_V + 1, :] for j in range(2)]
    o_ref[...] = jnp.concatenate(outs, axis=0).T


def _flash(q, k, vt, ccol, *, batch, seq, tq, fox):
    n = batch * seq
    nq = seq // tq
    pairs = MLA_HEADS // 2
    qw = LANES if fox else 2 * HEAD_BLOCK
    in_specs = [
        pl.BlockSpec((tq, qw), lambda b, hp, i: (b * nq + i, hp)),
        pl.BlockSpec((seq, qw), lambda b, hp, i: (b, hp)),
        pl.BlockSpec((1, LANES, seq), lambda b, hp, i: (b, hp, 0)),
    ]
    args = [q, k, vt]
    if fox:
        in_specs.append(pl.BlockSpec((seq, LANES), lambda b, hp, i: (b, hp)))
        args.append(ccol)
    return pl.pallas_call(
        functools.partial(_flash_body, tq=tq, fox=fox),
        grid=(batch, pairs, nq),
        in_specs=in_specs,
        out_specs=pl.BlockSpec((tq, LANES), lambda b, hp, i: (b * nq + i, hp)),
        out_shape=jax.ShapeDtypeStruct((n, FOX_WIDTH), F32),
        scratch_shapes=[pltpu.VMEM((2, 1, tq), F32), pltpu.VMEM((2, MLA_V + ONES_ROWS, tq), F32)],
        compiler_params=pltpu.CompilerParams(dimension_semantics=("arbitrary", "arbitrary", "arbitrary"),
                                             vmem_limit_bytes=VMEM_LIMIT),
        name="fox_prompt_attention" if fox else "mla_prompt_attention",
    )(*args)


def _merge_body(x_ref, p_ref, oa_ref, ob_ref, n1_ref, wg_ref, wua_ref, wub_ref, wo_ref, wple_ref, wpg_ref,
                fn_ref, y_ref):
    x = x_ref[...]
    xn = _rms(x, n1_ref[...]).astype(BF16)
    g = _dot(xn, wg_ref[...])
    ga = g[:, 0:512]
    gb = g[:, 512:1024]
    ma = g[:, 1024:2048]
    mb = g[:, 2048:3072]
    ua = (oa_ref[...] * (ga * _sigmoid(ga))).astype(BF16)
    ub = (ob_ref[...] * (gb * _sigmoid(gb))).astype(BF16)
    ya = _dot(ua, wua_ref[...])
    yb = _dot(ub, wub_ref[...])
    mixed = _sigmoid(ma) * ya + _sigmoid(mb) * yb
    h = x + _dot(mixed.astype(BF16), wo_ref[...])
    gate = _sigmoid(_dot(h.astype(BF16), wpg_ref[...]))
    h = h + gate * _dot(p_ref[...].astype(BF16), wple_ref[...])
    y_ref[...] = _rms(h, fn_ref[...])


def _merge(x2, p2, oa, ob, n1, wg, wua, wub, wo, wple, wpg, fn, *, tm):
    n = x2.shape[0]
    rows = lambda w: pl.BlockSpec((tm, w), lambda i: (i, 0))
    full = lambda a: pl.BlockSpec(a.shape, lambda i: (0, 0))
    return pl.pallas_call(
        _merge_body,
        grid=(n // tm,),
        in_specs=[rows(D_MODEL), rows(PLE_DIM), rows(512), rows(512), full(n1), full(wg), full(wua), full(wub),
                  full(wo), full(wple), full(wpg), full(fn)],
        out_specs=rows(D_MODEL),
        out_shape=jax.ShapeDtypeStruct((n, D_MODEL), F32),
        compiler_params=pltpu.CompilerParams(dimension_semantics=("arbitrary",), vmem_limit_bytes=VMEM_LIMIT),
        name="merge",
    )(x2, p2, oa, ob, n1, wg, wua, wub, wo, wple, wpg, fn)


def _proj_sample_body(x_ref, n1_ref, wa_ref, qn_ref, kvn_ref, wq1_ref, wq2_ref, wk_ref, bf_ref,
                      ckv_ref, kr_ref, fq_ref, fk_ref, fv_ref, logf_ref, qcat_ref, *, past_len):
    rows = x_ref.shape[0]
    pos = jnp.full((rows, 1), past_len, jnp.int32).astype(F32)
    cos_t, sin_t = _rope_tables(pos)
    xn = _rms(x_ref[...], n1_ref[...]).astype(BF16)
    q_heads, ckv, kr, fq, fk, fv, logf = _project_common(
        xn, wa_ref[...], qn_ref[...], kvn_ref[...], wq1_ref[...], wq2_ref[...], bf_ref[...], cos_t, sin_t)
    ckv_ref[...] = ckv
    kr_ref[...] = kr[:, :MLA_ROPE]
    fq_ref[...] = fq
    fk_ref[...] = fk
    fv_ref[...] = fv
    logf_ref[...] = logf
    lane = lax.broadcasted_iota(jnp.int32, (rows, LANES), 1)
    for h in range(MLA_HEADS):
        qh = q_heads[h]
        q_lat = _dot_nt(qh.astype(BF16), wk_ref[:, h * HEAD_BLOCK:(h + 1) * HEAD_BLOCK])
        qcat_ref[h, :, 0:LANES] = q_lat
        qcat_ref[h, :, LANES:2 * LANES] = jnp.where(lane < MLA_ROPE, qh, 0.0)


def _proj_sample(xs, n1, wa, qn, kvn, wq1, wq2, wk, bf, *, past_len):
    nb = xs.shape[0]
    full = lambda a: pl.BlockSpec(a.shape, lambda i: (0,) * a.ndim)
    out_shape = (
        jax.ShapeDtypeStruct((nb, MLA_KV_RANK), F32),
        jax.ShapeDtypeStruct((nb, MLA_ROPE), F32),
        jax.ShapeDtypeStruct((nb, FOX_WIDTH), F32),
        jax.ShapeDtypeStruct((nb, FOX_WIDTH), F32),
        jax.ShapeDtypeStruct((nb, FOX_WIDTH), F32),
        jax.ShapeDtypeStruct((nb, LANES), F32),
        jax.ShapeDtypeStruct((MLA_HEADS, nb, 2 * LANES), F32),
    )
    out_specs = tuple(pl.BlockSpec(s.shape, functools.partial(lambda nd, i: (0,) * nd, len(s.shape)))
                      for s in out_shape)
    return pl.pallas_call(
        functools.partial(_proj_sample_body, past_len=past_len),
        grid=(1,),
        in_specs=[full(xs), full(n1), full(wa), full(qn), full(kvn), full(wq1), full(wq2), full(wk), full(bf)],
        out_specs=out_specs,
        out_shape=out_shape,
        compiler_params=pltpu.CompilerParams(dimension_semantics=("arbitrary",), vmem_limit_bytes=VMEM_LIMIT),
        name="proj_sample",
    )(xs, n1, wa, qn, kvn, wq1, wq2, wk, bf)


PAGES_PER_STEP = 8
RING_SLOTS = 4


def _suffix_constants(page):
    p = np.arange(page)
    later = (p[:, None] > p[None, :]).astype(np.float32)
    return jnp.asarray(np.concatenate([later, np.ones((page, page), np.float32)], axis=1), BF16)


def _decode_body(pt_ref, fq_ref, qcat_ref, knew_ref, vnew_ref, lfnew_ref, ckvnew_ref, krnew_ref, wuv_ref,
                 uu_ref, kc_hbm, vc_hbm, ckv_hbm, kr_hbm, lf_hbm,
                 oa_ref, ob_ref,
                 kbuf, vbuf, cbuf, rbuf, lbuf, sems, mf, lf, accf, mm, lm, accm, aft, *, n_pages, page):
    g_pages = PAGES_PER_STEP
    nch = n_pages // g_pages
    t = pl.program_id(0)
    nt = pl.num_programs(0)
    c_idx = t % nch

    def copies(tt, slot):
        bb = tt // nch
        cc = nch - 1 - tt % nch
        out = []
        for g in range(g_pages):
            pg = pt_ref[bb * n_pages + cc * g_pages + g]
            lanes = pl.ds(g * page, page)
            out.append(pltpu.make_async_copy(kc_hbm.at[pg], kbuf.at[slot, :, lanes], sems.at[0, slot]))
            out.append(pltpu.make_async_copy(vc_hbm.at[pg], vbuf.at[slot, g], sems.at[1, slot]))
            out.append(pltpu.make_async_copy(ckv_hbm.at[pg], cbuf.at[slot, pl.ds(g * page, page), :],
                                             sems.at[2, slot]))
            out.append(pltpu.make_async_copy(kr_hbm.at[pg], rbuf.at[slot, :, lanes], sems.at[3, slot]))
            out.append(pltpu.make_async_copy(lf_hbm.at[pg], lbuf.at[slot, pl.ds(g * SUBLANES, SUBLANES), :],
                                             sems.at[4, slot]))
        return out

    slot = t % RING_SLOTS

    @pl.when(t == 0)
    def _():
        for ahead in range(RING_SLOTS - 1):
            for cp in copies(ahead, ahead):
                cp.start()

    @pl.when(t + RING_SLOTS - 1 < nt)
    def _():
        for cp in copies(t + RING_SLOTS - 1, (t + RING_SLOTS - 1) % RING_SLOTS):
            cp.start()

    for cp in copies(t, slot):
        cp.wait()

    row8 = lax.broadcasted_iota(jnp.int32, (FOX_HEADS, FOX_WIDTH), 0)
    col8 = lax.broadcasted_iota(jnp.int32, (FOX_HEADS, FOX_WIDTH), 1)
    diag = (col8 // FOX_HD) == row8
    qm = jnp.where(diag, jnp.broadcast_to(fq_ref[0], (FOX_HEADS, FOX_WIDTH)), 0.0)
    qcat = qcat_ref[0]
    q_lat = qcat[:, :MLA_KV_RANK]
    q_rope = qcat[:, MLA_KV_RANK:MLA_KV_RANK + MLA_ROPE]

    @pl.when(c_idx == 0)
    def _():
        mf[...] = jnp.full_like(mf, NEG)
        mm[...] = jnp.full_like(mm, NEG)
        lf[...] = jnp.zeros_like(lf)
        lm[...] = jnp.zeros_like(lm)
        accf[...] = jnp.zeros_like(accf)
        accm[...] = jnp.zeros_like(accm)
        aft[...] = jnp.zeros_like(aft)

    def softmax_step(m_ref, l_ref, s):
        m_old = m_ref[...]
        m_new = jnp.maximum(m_old, jnp.max(s, axis=1, keepdims=True))
        alpha = jnp.exp(m_old - m_new)
        p = jnp.exp(s - m_new[:, 0:1])
        l_ref[...] = l_ref[...] * alpha + jnp.sum(p, axis=1, keepdims=True)
        m_ref[...] = m_new
        return alpha, p

    z = _dot3_lhs(lbuf[slot], uu_ref[...])
    after = aft[...]
    biases = [None] * g_pages
    for g in range(g_pages - 1, -1, -1):
        rows = slice(g * SUBLANES, (g + 1) * SUBLANES)
        biases[g] = z[rows, :page] + after
        after = after + z[rows, page:]
    aft[...] = after
    s_f = _dot(qm, kbuf[slot]) + jnp.concatenate(biases, axis=1)
    alpha_f, p_f = softmax_step(mf, lf, s_f)
    for h in range(FOX_HEADS):
        rows = slice(h * FOX_HD, (h + 1) * FOX_HD)
        a = accf[rows, :] * alpha_f[h:h + 1, :]
        for g in range(g_pages):
            a = a + vbuf[slot, g, rows, :] * p_f[h:h + 1, g * page:(g + 1) * page]
        accf[rows, :] = a

    cpg = cbuf[slot]
    s_m = (_dot_nt(q_lat, cpg) + _dot(q_rope, rbuf[slot])) * MLA_SCALE
    alpha_m, p_m = softmax_step(mm, lm, s_m)
    accm[...] = accm[...] * alpha_m + _dot(p_m, cpg)

    @pl.when(c_idx == nch - 1)
    def _():
        def expand(col):
            wide = jnp.concatenate([col] * (FOX_WIDTH // LANES), axis=1)
            return jnp.sum(jnp.where(diag, wide, 0.0), axis=0, keepdims=True)

        s_new = jnp.sum(qm * knew_ref[0], axis=1, keepdims=True) - lfnew_ref[0]
        alpha, p_new = softmax_step(mf, lf, s_new[:, 0:1])
        past = _dot_nt(jnp.ones((SUBLANES, LANES), F32), accf[...])[0:1, :]
        ob_ref[0] = (past * expand(alpha) + expand(jnp.broadcast_to(p_new, (FOX_HEADS, LANES))) * vnew_ref[0]) \
            / expand(lf[...])
        s_new = (jnp.sum(q_lat * ckvnew_ref[0], axis=1, keepdims=True)
                 + jnp.sum(q_rope * krnew_ref[0], axis=1, keepdims=True)) * MLA_SCALE
        alpha, p_new = softmax_step(mm, lm, s_new)
        o_lat = (accm[...] * alpha + p_new * ckvnew_ref[0]) / lm[...]
        o_full = _dot(o_lat, wuv_ref[...])
        oa_ref[0] = jnp.sum(jnp.where(diag, o_full, 0.0), axis=0, keepdims=True)


def _decode(page_flat, fq3, qcat, knew3, vnew3, lfnew3, ckvnew3, krnew3, wuv, kc, vc, ckvc, krc, lfc,
            *, nb, n_pages, page):
    nch = n_pages // PAGES_PER_STEP
    uu = _suffix_constants(page)
    per_b = lambda shape: pl.BlockSpec((1,) + shape, lambda t, pt: (t // nch, 0, 0))
    const2 = lambda a: pl.BlockSpec(a.shape, lambda t, pt: (0, 0))
    anyspec = pl.BlockSpec(memory_space=pl.ANY)
    grid_spec = pltpu.PrefetchScalarGridSpec(
        num_scalar_prefetch=1,
        grid=(nb * nch,),
        in_specs=[per_b((1, FOX_WIDTH)), per_b((MLA_HEADS, 2 * LANES)), per_b((1, FOX_WIDTH)),
                  per_b((1, FOX_WIDTH)), per_b((FOX_HEADS, LANES)), per_b((1, MLA_KV_RANK)),
                  per_b((1, MLA_ROPE)), const2(wuv), const2(uu),
                  anyspec, anyspec, anyspec, anyspec, anyspec],
        out_specs=(per_b((1, FOX_WIDTH)), per_b((1, FOX_WIDTH))),
        scratch_shapes=[
            pltpu.VMEM((RING_SLOTS, FOX_WIDTH, PAGES_PER_STEP * page), F32),
            pltpu.VMEM((RING_SLOTS, PAGES_PER_STEP, FOX_WIDTH, page), F32),
            pltpu.VMEM((RING_SLOTS, PAGES_PER_STEP * page, MLA_KV_RANK), F32),
            pltpu.VMEM((RING_SLOTS, MLA_ROPE, PAGES_PER_STEP * page), F32),
            pltpu.VMEM((RING_SLOTS, PAGES_PER_STEP * SUBLANES, page), F32),
            pltpu.SemaphoreType.DMA((5, RING_SLOTS)),
            pltpu.VMEM((FOX_HEADS, LANES), F32), pltpu.VMEM((FOX_HEADS, LANES), F32),
            pltpu.VMEM((FOX_WIDTH, page), F32),
            pltpu.VMEM((MLA_HEADS, LANES), F32), pltpu.VMEM((MLA_HEADS, LANES), F32),
            pltpu.VMEM((MLA_HEADS, MLA_KV_RANK), F32),
            pltpu.VMEM((FOX_HEADS, LANES), F32),
        ],
    )
    return pl.pallas_call(
        functools.partial(_decode_body, n_pages=n_pages, page=page),
        grid_spec=grid_spec,
        out_shape=(jax.ShapeDtypeStruct((nb, 1, FOX_WIDTH), F32), jax.ShapeDtypeStruct((nb, 1, FOX_WIDTH), F32)),
        compiler_params=pltpu.CompilerParams(dimension_semantics=("arbitrary",), vmem_limit_bytes=VMEM_LIMIT),
        name="paged_decode_attention",
    )(page_flat, fq3, qcat, knew3, vnew3, lfnew3, ckvnew3, krnew3, wuv, uu, kc, vc, ckvc, krc, lfc)


def _pack_weights(w_in, w_uq, w_uk, w_uv, fox_forget_bias):
    o = np.cumsum((0, MLA_Q_RANK, MLA_KV_RANK, MLA_ROPE, FOX_WIDTH, FOX_WIDTH, FOX_WIDTH, FOX_HEADS,
                   512, 512, D_MODEL, D_MODEL))
    cq, ckv, kr, fq, fk, fv, fl, ga, gb, ma, mb = [w_in[:, o[i]:o[i + 1]] for i in range(11)]
    pad = lambda w, width: jnp.pad(w, ((0, 0), (0, width - w.shape[1])))
    k1, k2 = kr[:, :HALF], kr[:, HALF:]
    wa = jnp.concatenate([cq, ckv, pad(jnp.concatenate([k1, k2], 1), LANES),
                          pad(jnp.concatenate([k2, k1], 1), LANES), pad(fl, LANES), fq, fk, fv], axis=1)
    wg = jnp.concatenate([ga, gb, ma, mb], axis=1)
    uq = w_uq.reshape(MLA_Q_RANK, MLA_HEADS, MLA_NOPE + MLA_ROPE)
    nope, x1, x2 = uq[..., :MLA_NOPE], uq[..., MLA_NOPE:MLA_NOPE + HALF], uq[..., MLA_NOPE + HALF:]
    zpad = jnp.zeros((MLA_Q_RANK, MLA_HEADS, HEAD_BLOCK - MLA_ROPE - MLA_NOPE), w_uq.dtype)
    wq1 = jnp.concatenate([x1, x2, nope, zpad], axis=-1).reshape(MLA_Q_RANK, MLA_HEADS * HEAD_BLOCK)
    wq2 = jnp.concatenate([x2, x1, jnp.zeros_like(nope), zpad], axis=-1).reshape(MLA_Q_RANK,
                                                                                   MLA_HEADS * HEAD_BLOCK)
    zk = jnp.zeros((MLA_KV_RANK, MLA_HEADS, MLA_ROPE), w_uk.dtype)
    wk = jnp.concatenate([zk, w_uk, zk], axis=-1).reshape(MLA_KV_RANK, MLA_HEADS * HEAD_BLOCK)
    wv = w_uv.reshape(MLA_KV_RANK, MLA_HEADS * MLA_V)
    bf = pad(fox_forget_bias.reshape(1, FOX_HEADS), LANES)
    wkvt = jnp.concatenate([fk, fv], axis=1).T
    return wa.astype(BF16), wg.astype(BF16), wq1.astype(BF16), wq2.astype(BF16), wk.astype(BF16), \
        wv.astype(BF16), bf, wkvt.astype(BF16)


def kernel(x_prompt, x_sample, cache_mla_ckv, cache_mla_krope, cache_fox_k, cache_fox_v, cache_fox_logf,
           page_table, p_prompt, p_sample, norm1, w_in, q_norm, kv_norm, w_uq, w_uk, w_uv, fox_forget_bias,
           w_up_a, w_up_b, w_o, w_ple, w_ple_gate, final_norm):
    depth = norm1.shape[0]
    assert depth == 1, "single-layer step"
    batch, seq, _ = x_prompt.shape
    nb, dec_seq, _ = x_sample.shape
    assert dec_seq == 1, "one new token per decode sequence"
    n_pool, page = cache_mla_ckv.shape[1], cache_mla_ckv.shape[2]
    n_pages = page_table.shape[1]
    past_len = n_pages * page
    assert page == LANES and n_pages % PAGES_PER_STEP == 0

    wa, wg, wq1, wq2, wk, wv, bf, wkvt = _pack_weights(w_in[0], w_uq[0], w_uk[0], w_uv[0], fox_forget_bias[0])
    n1 = norm1[0].reshape(1, D_MODEL)
    qn = q_norm[0].reshape(1, MLA_Q_RANK)
    kvn = kv_norm[0].reshape(1, MLA_KV_RANK)
    fn = final_norm.reshape(1, D_MODEL)
    wua, wub, wo = w_up_a[0].astype(BF16), w_up_b[0].astype(BF16), w_o[0].astype(BF16)
    wple, wpg = w_ple[0].astype(BF16), w_ple_gate[0].astype(BF16)

    n = batch * seq
    x2 = x_prompt.reshape(n, D_MODEL)
    (q_full, k_full, vt_mla, ckv_p, kr_p, fq_b, fk_b, fvt_b, fkt_p, fvt_p, lf_p, ccol) = _proj_prompt(
        x2, n1, wa, qn, kvn, wq1, wq2, wk, wv.T, bf, wkvt, batch=batch, seq=seq, tm=256)
    o_a = _flash(q_full, k_full, vt_mla, None, batch=batch, seq=seq, tq=512, fox=False)
    o_b = _flash(fq_b, fk_b, fvt_b, ccol, batch=batch, seq=seq, tq=512, fox=True)
    to_heads = lambda a: jnp.transpose(a.reshape(batch, FOX_HEADS, FOX_HD, seq), (0, 3, 1, 2))[None]
    fk_p, fv_p = to_heads(fkt_p), to_heads(fvt_p)
    y_p = _merge(x2, p_prompt[0].reshape(n, PLE_DIM), o_a, o_b, n1, wg, wua, wub, wo, wple, wpg, fn, tm=256)

    xs = x_sample.reshape(nb, D_MODEL)
    ckv_s, kr_s, fq_s, fk_s, fv_s, lf_s, qcat = _proj_sample(xs, n1, wa, qn, kvn, wq1, wq2, wk, bf,
                                                              past_len=past_len)
    page_flat = page_table.reshape(nb * n_pages)
    lf_cols = jnp.broadcast_to(lf_s[:, :FOX_HEADS, None], (nb, FOX_HEADS, LANES))
    kc = jnp.transpose(cache_fox_k[0], (0, 2, 3, 1)).reshape(n_pool, FOX_WIDTH, page)
    vc = jnp.transpose(cache_fox_v[0], (0, 2, 3, 1)).reshape(n_pool, FOX_WIDTH, page)
    krc = jnp.transpose(cache_mla_krope[0], (0, 2, 1))
    lfc = jnp.transpose(cache_fox_logf[0], (0, 2, 1))
    oa_s, ob_s = _decode(
        page_flat, fq_s.reshape(nb, 1, FOX_WIDTH), jnp.transpose(qcat, (1, 0, 2)),
        fk_s.reshape(nb, 1, FOX_WIDTH), fv_s.reshape(nb, 1, FOX_WIDTH), lf_cols,
        ckv_s.reshape(nb, 1, MLA_KV_RANK), kr_s.reshape(nb, 1, MLA_ROPE), wv,
        kc, vc, cache_mla_ckv[0], krc, lfc, nb=nb, n_pages=n_pages, page=page)
    y_s = _merge(xs, p_sample[0].reshape(nb, PLE_DIM), oa_s.reshape(nb, 512), ob_s.reshape(nb, 512),
                 n1, wg, wua, wub, wo, wple, wpg, fn, tm=nb)

    return (y_p.reshape(batch, seq, D_MODEL), y_s.reshape(nb, 1, D_MODEL),
            ckv_p.reshape(1, batch, seq, MLA_KV_RANK), kr_p.reshape(1, batch, seq, MLA_ROPE),
            fk_p, fv_p, lf_p.reshape(1, batch, seq, FOX_HEADS),
            ckv_s.reshape(1, nb, 1, MLA_KV_RANK), kr_s.reshape(1, nb, 1, MLA_ROPE),
            fk_s.reshape(1, nb, 1, FOX_HEADS, FOX_HD), fv_s.reshape(1, nb, 1, FOX_HEADS, FOX_HD),
            lf_s[:, :FOX_HEADS].reshape(1, nb, 1, FOX_HEADS))
```

```python
import functools
import math

import jax
import jax.numpy as jnp
import numpy as np
from jax import lax
from jax.experimental import pallas as pl
from jax.experimental.pallas import tpu as pltpu

D_MODEL = 1024
MLA_HEADS = 8
MLA_NOPE = 64
MLA_ROPE = 32
MLA_V = 64
MLA_Q_RANK = 256
MLA_KV_RANK = 128
MLA_SCALE = (MLA_NOPE + MLA_ROPE) ** -0.5
FOX_HEADS = 8
FOX_HD = 64
FOX_WIDTH = FOX_HEADS * FOX_HD
FOX_SCALE = FOX_HD ** -0.5
PLE_DIM = 256
ROPE_THETA = 10000.0
EPS = 1e-6
NEG = -1e30
HALF = MLA_ROPE // 2
LOG2E = math.log2(math.e)

LANES = 128
SUBLANES = 8
HEAD_BLOCK = LANES
VMEM_LIMIT = 56 * 1024 * 1024

A_CQ, A_CKV, A_KRA, A_KRB, A_FL, A_FQ, A_FK, A_FV, A_END = 0, 256, 384, 512, 640, 768, 1280, 1792, 2304

F32 = jnp.float32
BF16 = jnp.bfloat16


def _dot(a, b):
    return jnp.dot(a, b, preferred_element_type=F32)


def _dot_nt(a, b):
    return lax.dot_general(a, b, (((1,), (1,)), ((), ())), preferred_element_type=F32)


def _split3(x):
    hi = x.astype(BF16)
    r = x - hi.astype(F32)
    mid = r.astype(BF16)
    lo = (r - mid.astype(F32)).astype(BF16)
    return hi, mid, lo


def _dot3_rhs(a_bf16, x):
    hi, mid, lo = _split3(x)
    return _dot(a_bf16, lo) + _dot(a_bf16, mid) + _dot(a_bf16, hi)


def _dot3_lhs(x, b_bf16):
    hi, mid, lo = _split3(x)
    return _dot(lo, b_bf16) + _dot(mid, b_bf16) + _dot(hi, b_bf16)


def _rms(x, g):
    inv = lax.rsqrt(jnp.mean(x * x, axis=-1, keepdims=True) + EPS)
    return (x * inv) * g


def _sigmoid(x):
    return 1.0 / (1.0 + jnp.exp(-x))


def _log_sigmoid(x):
    return -(jnp.maximum(-x, 0.0) + jnp.log(1.0 + jnp.exp(-jnp.abs(x))))


def _rope_tables(pos_f32):
    rows = pos_f32.shape[0]
    lane = lax.broadcasted_iota(jnp.int32, (rows, LANES), 1)
    idx = (lane % HALF).astype(F32)
    inv_freq = jnp.exp(-math.log(ROPE_THETA) * idx / HALF)
    ang = pos_f32 * inv_freq
    cos = jnp.cos(ang)
    sin = jnp.sin(ang)
    cos_t = jnp.where(lane < MLA_ROPE, cos, jnp.where(lane < MLA_ROPE + MLA_NOPE, 1.0, 0.0))
    sin_t = jnp.where(lane < HALF, -sin, jnp.where(lane < MLA_ROPE, sin, 0.0))
    return cos_t, sin_t


def _project_common(xn, wa, qn, kvn, wq1, wq2, bf, cos_t, sin_t):
    z = _dot(xn, wa)
    cqn = _rms(z[:, A_CQ:A_CKV], qn).astype(BF16)
    q1 = _dot(cqn, wq1)
    q2 = _dot(cqn, wq2)
    q_heads = []
    for h in range(MLA_HEADS):
        sl = slice(h * HEAD_BLOCK, (h + 1) * HEAD_BLOCK)
        q_heads.append(q1[:, sl] * cos_t + q2[:, sl] * sin_t)
    ckv = _rms(z[:, A_CKV:A_KRA], kvn)
    kr = z[:, A_KRA:A_KRB] * cos_t + z[:, A_KRB:A_FL] * sin_t
    logf = _log_sigmoid(z[:, A_FL:A_FQ] + bf)
    fq = z[:, A_FQ:A_FK] * FOX_SCALE
    fk = z[:, A_FK:A_FV]
    fv = z[:, A_FV:A_END]
    return q_heads, ckv, kr, fq, fk, fv, logf


def _rope_table_body(cos_ref, sin_ref):
    rows = cos_ref.shape[0]
    pos = lax.broadcasted_iota(jnp.int32, (rows, 1), 0).astype(F32)
    cos_ref[...], sin_ref[...] = _rope_tables(pos)


def _rope_table(seq):
    shape = jax.ShapeDtypeStruct((seq, LANES), F32)
    spec = pl.BlockSpec((seq, LANES), lambda i: (0, 0))
    return pl.pallas_call(_rope_table_body, grid=(1,), out_specs=(spec, spec), out_shape=(shape, shape),
                          name="rope_table")()


def _proj_prompt_body(x_ref, cos_ref, sin_ref, n1_ref, wa_ref, qn_ref, kvn_ref, wq1_ref, wq2_ref, wk_ref, wvt_ref,
                      bf_ref, wkvt_ref, rep_ref,
                      qfull_ref, kfull_ref, vmlat_ref, ckv_ref, kr_ref, fq_ref, fkb_ref, fvtb_ref,
                      fkt_ref, fvt_ref, logf_ref, ccol_ref, carry_ref, *, tm):
    s_idx = pl.program_id(1)
    cos_t, sin_t = cos_ref[...], sin_ref[...]
    xn = _rms(x_ref[...], n1_ref[...]).astype(BF16)
    q_heads, ckv, kr, fq, fk, fv, logf = _project_common(
        xn, wa_ref[...], qn_ref[...], kvn_ref[...], wq1_ref[...], wq2_ref[...], bf_ref[...], cos_t, sin_t)
    for h in range(MLA_HEADS):
        qfull_ref[:, h * HEAD_BLOCK:(h + 1) * HEAD_BLOCK] = (q_heads[h] * (MLA_SCALE * LOG2E)).astype(BF16)
    ckv_ref[...] = ckv
    kr_ref[...] = kr[:, :MLA_ROPE]
    ckv_b = ckv.astype(BF16)
    kn = _dot(ckv_b, wk_ref[...])
    for h in range(MLA_HEADS):
        sl = slice(h * HEAD_BLOCK, (h + 1) * HEAD_BLOCK)
        kfull_ref[:, sl] = (kn[:, sl] + kr).astype(BF16)
    vmlat_ref[0] = _dot_nt(wvt_ref[...], ckv_b).astype(BF16)
    fq_ref[...] = (fq * LOG2E).astype(BF16)
    fkb_ref[...] = fk.astype(BF16)
    kvt = _dot_nt(wkvt_ref[...], xn)
    fkt_ref[0] = kvt[:FOX_WIDTH, :]
    fvt_ref[0] = kvt[FOX_WIDTH:, :]
    fvtb_ref[0] = kvt[FOX_WIDTH:, :].astype(BF16)
    logf_ref[...] = logf[:, :FOX_HEADS]

    @pl.when(s_idx == 0)
    def _():
        carry_ref[...] = jnp.zeros_like(carry_ref)

    row = lax.broadcasted_iota(jnp.int32, (tm, tm), 0)
    col = lax.broadcasted_iota(jnp.int32, (tm, tm), 1)
    tri = jnp.where(row >= col, 1.0, 0.0).astype(BF16)
    cum = _dot3_rhs(tri, logf) + carry_ref[0:1, :]
    carry_ref[0:1, :] = cum[tm - 1:tm, :]
    parts = _split3(-(cum * LOG2E))
    ccol_ref[...] = sum(_dot(parts[i], rep_ref[i]) for i in range(3)).astype(BF16)


BIAS_PARTS = 3


def _pair_columns():
    rep = np.zeros((BIAS_PARTS, LANES, (FOX_HEADS // 2) * LANES), np.float32)
    for i in range(BIAS_PARTS):
        for h in range(FOX_HEADS):
            rep[i, h, (h // 2) * LANES + BIAS_PARTS * (h % 2) + i] = 1.0
    return jnp.asarray(rep, BF16)


def _proj_prompt(x2, n1, wa, qn, kvn, wq1, wq2, wk, wvt, bf, wkvt, *, batch, seq, tm):
    n = batch * seq
    ns = seq // tm
    row_map = lambda b, s: (b * ns + s, 0)
    const = lambda b, s: (0, 0)

    def rows(w):
        return pl.BlockSpec((tm, w), row_map)

    def full(a):
        return pl.BlockSpec(a.shape, const)

    rep = _pair_columns()
    cos_t, sin_t = _rope_table(seq)
    table = pl.BlockSpec((tm, LANES), lambda b, s: (s, 0))
    out_shape = (
        jax.ShapeDtypeStruct((n, MLA_HEADS * HEAD_BLOCK), BF16),
        jax.ShapeDtypeStruct((n, MLA_HEADS * HEAD_BLOCK), BF16),
        jax.ShapeDtypeStruct((batch, MLA_HEADS * MLA_V, seq), BF16),
        jax.ShapeDtypeStruct((n, MLA_KV_RANK), F32),
        jax.ShapeDtypeStruct((n, MLA_ROPE), F32),
        jax.ShapeDtypeStruct((n, FOX_WIDTH), BF16),
        jax.ShapeDtypeStruct((n, FOX_WIDTH), BF16),
        jax.ShapeDtypeStruct((batch, FOX_WIDTH, seq), BF16),
        jax.ShapeDtypeStruct((batch, FOX_WIDTH, seq), F32),
        jax.ShapeDtypeStruct((batch, FOX_WIDTH, seq), F32),
        jax.ShapeDtypeStruct((n, FOX_HEADS), F32),
        jax.ShapeDtypeStruct((n, (FOX_HEADS // 2) * LANES), BF16),
    )
    cols = lambda r: pl.BlockSpec((1, r, tm), lambda b, s: (b, 0, s))
    out_specs = (
        rows(MLA_HEADS * HEAD_BLOCK), rows(MLA_HEADS * HEAD_BLOCK), cols(MLA_HEADS * MLA_V),
        rows(MLA_KV_RANK), rows(MLA_ROPE), rows(FOX_WIDTH), rows(FOX_WIDTH), cols(FOX_WIDTH),
        cols(FOX_WIDTH), cols(FOX_WIDTH), rows(FOX_HEADS), rows((FOX_HEADS // 2) * LANES),
    )
    return pl.pallas_call(
        functools.partial(_proj_prompt_body, tm=tm),
        grid=(batch, ns),
        in_specs=[rows(D_MODEL), table, table, full(n1), full(wa), full(qn), full(kvn), full(wq1), full(wq2),
                  full(wk), full(wvt), full(bf), full(wkvt), pl.BlockSpec(rep.shape, lambda b, s: (0, 0, 0))],
        out_specs=out_specs,
        out_shape=out_shape,
        scratch_shapes=[pltpu.VMEM((SUBLANES, LANES), F32)],
        compiler_params=pltpu.CompilerParams(dimension_semantics=("arbitrary", "arbitrary"),
                                             vmem_limit_bytes=VMEM_LIMIT),
        name="proj_prompt",
    )(x2, cos_t, sin_t, n1, wa, qn, kvn, wq1, wq2, wk, wvt, bf, wkvt, rep)


ONES_ROWS = 16


def _flash_body(*refs, tq, fox):
    if fox:
        q_ref, k_ref, vt_ref, c_ref, o_ref, m_sc, acc_sc, s_a, s_b = refs
    else:
        q_ref, k_ref, vt_ref, o_ref, m_sc, acc_sc, s_a, s_b = refs
    qi = pl.program_id(2)
    tk = tq
    if fox:
        qv = q_ref[...]
        zero = jnp.zeros_like(qv)
        lane = lax.broadcasted_iota(jnp.int32, (tq, LANES), 1)
        low = lane < FOX_HD
        qs = tuple(
            jnp.concatenate(
                [jnp.where(low if j == 0 else jnp.logical_not(low), qv, zero),
                 jnp.where((lane >= BIAS_PARTS * j) & (lane < BIAS_PARTS * (j + 1)), 1.0, 0.0).astype(BF16)],
                axis=1)
            for j in range(2))
        ksl = (slice(0, LANES), slice(0, LANES))
    else:
        qs = (q_ref[:, 0:HEAD_BLOCK], q_ref[:, HEAD_BLOCK:2 * HEAD_BLOCK])
        ksl = (slice(0, HEAD_BLOCK), slice(HEAD_BLOCK, 2 * HEAD_BLOCK))
    m_sc[...] = jnp.full_like(m_sc, NEG)
    acc_sc[...] = jnp.zeros_like(acc_sc)
    ones = jnp.ones((ONES_ROWS, tk), BF16)

    def scores(s_ref, kt):
        start = pl.multiple_of(kt * tk, tk)
        for j in range(2):
            k = k_ref[pl.ds(start, tk), ksl[j]]
            if fox:
                k = jnp.concatenate([k, c_ref[pl.ds(start, tk), :]], axis=1)
            s_ref[j] = _dot_nt(k, qs[j])

    def consume(s_ref, kt, masked):
        start = pl.multiple_of(kt * tk, tk)
        for j in range(2):
            st = s_ref[j]
            if masked:
                r = lax.broadcasted_iota(jnp.int32, (tk, tq), 0)
                c = lax.broadcasted_iota(jnp.int32, (tk, tq), 1)
                st = jnp.where(r <= c, st, NEG)
            m_old = m_sc[j]
            m_new = jnp.maximum(m_old, jnp.max(st, axis=0, keepdims=True))
            alpha = jnp.exp2(m_old - m_new)
            pt = jnp.exp2(st - m_new).astype(BF16)
            vt = jnp.concatenate([vt_ref[0, j * MLA_V:(j + 1) * MLA_V, pl.ds(start, tk)], ones], axis=0)
            acc_sc[j] = acc_sc[j] * alpha + _dot(vt, pt)
            m_sc[j] = m_new

    scores(s_a, 0)

    def body(i, carry):
        scores(s_b, 2 * i + 1)
        consume(s_a, 2 * i, False)
        scores(s_a, 2 * i + 2)
        consume(s_b, 2 * i + 1, False)
        return carry

    pairs_done = qi // 2
    lax.fori_loop(0, pairs_done, body, 0)

    @pl.when(qi % 2 == 1)
    def _():
        scores(s_b, qi)
        consume(s_a, qi - 1, False)
        consume(s_b, qi, True)

    @pl.when(qi % 2 == 0)
    def _():
        consume(s_a, qi, True)

    outs = [acc_sc[j, :MLA_V, :] / acc_sc[j, MLA_V:MLA_V + 1, :] for j in range(2)]
    o_ref[...] = jnp.concatenate(outs, axis=0).T


def _flash(q, k, vt, ccol, *, batch, seq, tq, fox):
    n = batch * seq
    nq = seq // tq
    pairs = MLA_HEADS // 2
    qw = LANES if fox else 2 * HEAD_BLOCK
    in_specs = [
        pl.BlockSpec((tq, qw), lambda b, hp, i: (b * nq + i, hp)),
        pl.BlockSpec((seq, qw), lambda b, hp, i: (b, hp)),
        pl.BlockSpec((1, LANES, seq), lambda b, hp, i: (b, hp, 0)),
    ]
    args = [q, k, vt]
    if fox:
        in_specs.append(pl.BlockSpec((seq, LANES), lambda b, hp, i: (b, hp)))
        args.append(ccol)
    return pl.pallas_call(
        functools.partial(_flash_body, tq=tq, fox=fox),
        grid=(batch, pairs, nq),
        in_specs=in_specs,
        out_specs=pl.BlockSpec((tq, LANES), lambda b, hp, i: (b * nq + i, hp)),
        out_shape=jax.ShapeDtypeStruct((n, FOX_WIDTH), F32),
        scratch_shapes=[pltpu.VMEM((2, 1, tq), F32), pltpu.VMEM((2, MLA_V + ONES_ROWS, tq), F32),
                        pltpu.VMEM((2, tq, tq), F32), pltpu.VMEM((2, tq, tq), F32)],
        compiler_params=pltpu.CompilerParams(dimension_semantics=("arbitrary", "arbitrary", "arbitrary"),
                                             vmem_limit_bytes=VMEM_LIMIT),
        name="fox_prompt_attention" if fox else "mla_prompt_attention",
    )(*args)


def _merge_body(x_ref, p_ref, oa_ref, ob_ref, n1_ref, wg_ref, wua_ref, wub_ref, wo_ref, wple_ref, wpg_ref,
                fn_ref, y_ref):
    x = x_ref[...]
    xn = _rms(x, n1_ref[...]).astype(BF16)
    g = _dot(xn, wg_ref[...])
    ga = g[:, 0:512]
    gb = g[:, 512:1024]
    ma = g[:, 1024:2048]
    mb = g[:, 2048:3072]
    ua = (oa_ref[...] * (ga * _sigmoid(ga))).astype(BF16)
    ub = (ob_ref[...] * (gb * _sigmoid(gb))).astype(BF16)
    ya = _dot(ua, wua_ref[...])
    yb = _dot(ub, wub_ref[...])
    mixed = _sigmoid(ma) * ya + _sigmoid(mb) * yb
    h = x + _dot(mixed.astype(BF16), wo_ref[...])
    gate = _sigmoid(_dot(h.astype(BF16), wpg_ref[...]))
    h = h + gate * _dot(p_ref[...].astype(BF16), wple_ref[...])
    y_ref[...] = _rms(h, fn_ref[...])


def _merge(x2, p2, oa, ob, n1, wg, wua, wub, wo, wple, wpg, fn, *, tm):
    n = x2.shape[0]
    rows = lambda w: pl.BlockSpec((tm, w), lambda i: (i, 0))
    full = lambda a: pl.BlockSpec(a.shape, lambda i: (0, 0))
    return pl.pallas_call(
        _merge_body,
        grid=(n // tm,),
        in_specs=[rows(D_MODEL), rows(PLE_DIM), rows(512), rows(512), full(n1), full(wg), full(wua), full(wub),
                  full(wo), full(wple), full(wpg), full(fn)],
        out_specs=rows(D_MODEL),
        out_shape=jax.ShapeDtypeStruct((n, D_MODEL), F32),
        compiler_params=pltpu.CompilerParams(dimension_semantics=("arbitrary",), vmem_limit_bytes=VMEM_LIMIT),
        name="merge",
    )(x2, p2, oa, ob, n1, wg, wua, wub, wo, wple, wpg, fn)


def _proj_sample_body(x_ref, n1_ref, wa_ref, qn_ref, kvn_ref, wq1_ref, wq2_ref, wk_ref, bf_ref,
                      ckv_ref, kr_ref, fq_ref, fk_ref, fv_ref, logf_ref, qcat_ref, *, past_len):
    rows = x_ref.shape[0]
    pos = jnp.full((rows, 1), past_len, jnp.int32).astype(F32)
    cos_t, sin_t = _rope_tables(pos)
    xn = _rms(x_ref[...], n1_ref[...]).astype(BF16)
    q_heads, ckv, kr, fq, fk, fv, logf = _project_common(
        xn, wa_ref[...], qn_ref[...], kvn_ref[...], wq1_ref[...], wq2_ref[...], bf_ref[...], cos_t, sin_t)
    ckv_ref[...] = ckv
    kr_ref[...] = kr[:, :MLA_ROPE]
    fq_ref[...] = fq
    fk_ref[...] = fk
    fv_ref[...] = fv
    logf_ref[...] = logf
    lane = lax.broadcasted_iota(jnp.int32, (rows, LANES), 1)
    for h in range(MLA_HEADS):
        qh = q_heads[h]
        q_lat = _dot_nt(qh.astype(BF16), wk_ref[:, h * HEAD_BLOCK:(h + 1) * HEAD_BLOCK])
        qcat_ref[h, :, 0:LANES] = q_lat
        qcat_ref[h, :, LANES:2 * LANES] = jnp.where(lane < MLA_ROPE, qh, 0.0)


def _proj_sample(xs, n1, wa, qn, kvn, wq1, wq2, wk, bf, *, past_len):
    nb = xs.shape[0]
    full = lambda a: pl.BlockSpec(a.shape, lambda i: (0,) * a.ndim)
    out_shape = (
        jax.ShapeDtypeStruct((nb, MLA_KV_RANK), F32),
        jax.ShapeDtypeStruct((nb, MLA_ROPE), F32),
        jax.ShapeDtypeStruct((nb, FOX_WIDTH), F32),
        jax.ShapeDtypeStruct((nb, FOX_WIDTH), F32),
        jax.ShapeDtypeStruct((nb, FOX_WIDTH), F32),
        jax.ShapeDtypeStruct((nb, LANES), F32),
        jax.ShapeDtypeStruct((MLA_HEADS, nb, 2 * LANES), F32),
    )
    out_specs = tuple(pl.BlockSpec(s.shape, functools.partial(lambda nd, i: (0,) * nd, len(s.shape)))
                      for s in out_shape)
    return pl.pallas_call(
        functools.partial(_proj_sample_body, past_len=past_len),
        grid=(1,),
        in_specs=[full(xs), full(n1), full(wa), full(qn), full(kvn), full(wq1), full(wq2), full(wk), full(bf)],
        out_specs=out_specs,
        out_shape=out_shape,
        compiler_params=pltpu.CompilerParams(dimension_semantics=("arbitrary",), vmem_limit_bytes=VMEM_LIMIT),
        name="proj_sample",
    )(xs, n1, wa, qn, kvn, wq1, wq2, wk, bf)


PAGES_PER_STEP = 8
RING_SLOTS = 4


def _suffix_constants(page):
    p = np.arange(page)
    later = (p[:, None] > p[None, :]).astype(np.float32)
    return jnp.asarray(np.concatenate([later, np.ones((page, page), np.float32)], axis=1), BF16)


def _decode_body(pt_ref, fq_ref, qcat_ref, knew_ref, vnew_ref, lfnew_ref, ckvnew_ref, krnew_ref, wuv_ref,
                 uu_ref, kc_hbm, vc_hbm, ckv_hbm, kr_hbm, lf_hbm,
                 oa_ref, ob_ref,
                 kbuf, vbuf, cbuf, rbuf, lbuf, sems, mf, lf, accf, mm, lm, accm, aft, *, n_pages, page):
    g_pages = PAGES_PER_STEP
    nch = n_pages // g_pages
    t = pl.program_id(0)
    nt = pl.num_programs(0)
    c_idx = t % nch

    def copies(tt, slot):
        bb = tt // nch
        cc = nch - 1 - tt % nch
        out = []
        for g in range(g_pages):
            pg = pt_ref[bb * n_pages + cc * g_pages + g]
            lanes = pl.ds(g * page, page)
            out.append(pltpu.make_async_copy(kc_hbm.at[pg], kbuf.at[slot, :, lanes], sems.at[0, slot]))
            out.append(pltpu.make_async_copy(vc_hbm.at[pg], vbuf.at[slot, g], sems.at[1, slot]))
            out.append(pltpu.make_async_copy(ckv_hbm.at[pg], cbuf.at[slot, pl.ds(g * page, page), :],
                                             sems.at[2, slot]))
            out.append(pltpu.make_async_copy(kr_hbm.at[pg], rbuf.at[slot, :, lanes], sems.at[3, slot]))
            out.append(pltpu.make_async_copy(lf_hbm.at[pg], lbuf.at[slot, pl.ds(g * SUBLANES, SUBLANES), :],
                                             sems.at[4, slot]))
        return out

    slot = t % RING_SLOTS

    @pl.when(t == 0)
    def _():
        for ahead in range(RING_SLOTS - 1):
            for cp in copies(ahead, ahead):
                cp.start()

    @pl.when(t + RING_SLOTS - 1 < nt)
    def _():
        for cp in copies(t + RING_SLOTS - 1, (t + RING_SLOTS - 1) % RING_SLOTS):
            cp.start()

    for cp in copies(t, slot):
        cp.wait()

    row8 = lax.broadcasted_iota(jnp.int32, (FOX_HEADS, FOX_WIDTH), 0)
    col8 = lax.broadcasted_iota(jnp.int32, (FOX_HEADS, FOX_WIDTH), 1)
    diag = (col8 // FOX_HD) == row8
    qm = jnp.where(diag, jnp.broadcast_to(fq_ref[0], (FOX_HEADS, FOX_WIDTH)), 0.0)
    qcat = qcat_ref[0]
    q_lat = qcat[:, :MLA_KV_RANK]
    q_rope = qcat[:, MLA_KV_RANK:MLA_KV_RANK + MLA_ROPE]

    @pl.when(c_idx == 0)
    def _():
        mf[...] = jnp.full_like(mf, NEG)
        mm[...] = jnp.full_like(mm, NEG)
        lf[...] = jnp.zeros_like(lf)
        lm[...] = jnp.zeros_like(lm)
        accf[...] = jnp.zeros_like(accf)
        accm[...] = jnp.zeros_like(accm)
        aft[...] = jnp.zeros_like(aft)

    def softmax_step(m_ref, l_ref, s):
        m_old = m_ref[...]
        m_new = jnp.maximum(m_old, jnp.max(s, axis=1, keepdims=True))
        alpha = jnp.exp(m_old - m_new)
        p = jnp.exp(s - m_new[:, 0:1])
        l_ref[...] = l_ref[...] * alpha + jnp.sum(p, axis=1, keepdims=True)
        m_ref[...] = m_new
        return alpha, p

    z = _dot3_lhs(lbuf[slot], uu_ref[...])
    after = aft[...]
    biases = [None] * g_pages
    for g in range(g_pages - 1, -1, -1):
        rows = slice(g * SUBLANES, (g + 1) * SUBLANES)
        biases[g] = z[rows, :page] + after
        after = after + z[rows, page:]
    aft[...] = after
    s_f = _dot(qm, kbuf[slot]) + jnp.concatenate(biases, axis=1)
    alpha_f, p_f = softmax_step(mf, lf, s_f)
    for h in range(FOX_HEADS):
        rows = slice(h * FOX_HD, (h + 1) * FOX_HD)
        a = accf[rows, :] * alpha_f[h:h + 1, :]
        for g in range(g_pages):
            a = a + vbuf[slot, g, rows, :] * p_f[h:h + 1, g * page:(g + 1) * page]
        accf[rows, :] = a

    cpg = cbuf[slot]
    s_m = (_dot_nt(q_lat, cpg) + _dot(q_rope, rbuf[slot])) * MLA_SCALE
    alpha_m, p_m = softmax_step(mm, lm, s_m)
    accm[...] = accm[...] * alpha_m + _dot(p_m, cpg)

    @pl.when(c_idx == nch - 1)
    def _():
        def expand(col):
            wide = jnp.concatenate([col] * (FOX_WIDTH // LANES), axis=1)
            return jnp.sum(jnp.where(diag, wide, 0.0), axis=0, keepdims=True)

        s_new = jnp.sum(qm * knew_ref[0], axis=1, keepdims=True) - lfnew_ref[0]
        alpha, p_new = softmax_step(mf, lf, s_new[:, 0:1])
        past = _dot_nt(jnp.ones((SUBLANES, LANES), F32), accf[...])[0:1, :]
        ob_ref[0] = (past * expand(alpha) + expand(jnp.broadcast_to(p_new, (FOX_HEADS, LANES))) * vnew_ref[0]) \
            / expand(lf[...])
        s_new = (jnp.sum(q_lat * ckvnew_ref[0], axis=1, keepdims=True)
                 + jnp.sum(q_rope * krnew_ref[0], axis=1, keepdims=True)) * MLA_SCALE
        alpha, p_new = softmax_step(mm, lm, s_new)
        o_lat = (accm[...] * alpha + p_new * ckvnew_ref[0]) / lm[...]
        o_full = _dot(o_lat, wuv_ref[...])
        oa_ref[0] = jnp.sum(jnp.where(diag, o_full, 0.0), axis=0, keepdims=True)


def _decode(page_flat, fq3, qcat, knew3, vnew3, lfnew3, ckvnew3, krnew3, wuv, kc, vc, ckvc, krc, lfc,
            *, nb, n_pages, page):
    nch = n_pages // PAGES_PER_STEP
    uu = _suffix_constants(page)
    per_b = lambda shape: pl.BlockSpec((1,) + shape, lambda t, pt: (t // nch, 0, 0))
    const2 = lambda a: pl.BlockSpec(a.shape, lambda t, pt: (0, 0))
    anyspec = pl.BlockSpec(memory_space=pl.ANY)
    grid_spec = pltpu.PrefetchScalarGridSpec(
        num_scalar_prefetch=1,
        grid=(nb * nch,),
        in_specs=[per_b((1, FOX_WIDTH)), per_b((MLA_HEADS, 2 * LANES)), per_b((1, FOX_WIDTH)),
                  per_b((1, FOX_WIDTH)), per_b((FOX_HEADS, LANES)), per_b((1, MLA_KV_RANK)),
                  per_b((1, MLA_ROPE)), const2(wuv), const2(uu),
                  anyspec, anyspec, anyspec, anyspec, anyspec],
        out_specs=(per_b((1, FOX_WIDTH)), per_b((1, FOX_WIDTH))),
        scratch_shapes=[
            pltpu.VMEM((RING_SLOTS, FOX_WIDTH, PAGES_PER_STEP * page), F32),
            pltpu.VMEM((RING_SLOTS, PAGES_PER_STEP, FOX_WIDTH, page), F32),
            pltpu.VMEM((RING_SLOTS, PAGES_PER_STEP * page, MLA_KV_RANK), F32),
            pltpu.VMEM((RING_SLOTS, MLA_ROPE, PAGES_PER_STEP * page), F32),
            pltpu.VMEM((RING_SLOTS, PAGES_PER_STEP * SUBLANES, page), F32),
            pltpu.SemaphoreType.DMA((5, RING_SLOTS)),
            pltpu.VMEM((FOX_HEADS, LANES), F32), pltpu.VMEM((FOX_HEADS, LANES), F32),
            pltpu.VMEM((FOX_WIDTH, page), F32),
            pltpu.VMEM((MLA_HEADS, LANES), F32), pltpu.VMEM((MLA_HEADS, LANES), F32),
            pltpu.VMEM((MLA_HEADS, MLA_KV_RANK), F32),
            pltpu.VMEM((FOX_HEADS, LANES), F32),
        ],
    )
    return pl.pallas_call(
        functools.partial(_decode_body, n_pages=n_pages, page=page),
        grid_spec=grid_spec,
        out_shape=(jax.ShapeDtypeStruct((nb, 1, FOX_WIDTH), F32), jax.ShapeDtypeStruct((nb, 1, FOX_WIDTH), F32)),
        compiler_params=pltpu.CompilerParams(dimension_semantics=("arbitrary",), vmem_limit_bytes=VMEM_LIMIT),
        name="paged_decode_attention",
    )(page_flat, fq3, qcat, knew3, vnew3, lfnew3, ckvnew3, krnew3, wuv, uu, kc, vc, ckvc, krc, lfc)


def _pack_weights(w_in, w_uq, w_uk, w_uv, fox_forget_bias):
    o = np.cumsum((0, MLA_Q_RANK, MLA_KV_RANK, MLA_ROPE, FOX_WIDTH, FOX_WIDTH, FOX_WIDTH, FOX_HEADS,
                   512, 512, D_MODEL, D_MODEL))
    cq, ckv, kr, fq, fk, fv, fl, ga, gb, ma, mb = [w_in[:, o[i]:o[i + 1]] for i in range(11)]
    pad = lambda w, width: jnp.pad(w, ((0, 0), (0, width - w.shape[1])))
    k1, k2 = kr[:, :HALF], kr[:, HALF:]
    wa = jnp.concatenate([cq, ckv, pad(jnp.concatenate([k1, k2], 1), LANES),
                          pad(jnp.concatenate([k2, k1], 1), LANES), pad(fl, LANES), fq, fk, fv], axis=1)
    wg = jnp.concatenate([ga, gb, ma, mb], axis=1)
    uq = w_uq.reshape(MLA_Q_RANK, MLA_HEADS, MLA_NOPE + MLA_ROPE)
    nope, x1, x2 = uq[..., :MLA_NOPE], uq[..., MLA_NOPE:MLA_NOPE + HALF], uq[..., MLA_NOPE + HALF:]
    zpad = jnp.zeros((MLA_Q_RANK, MLA_HEADS, HEAD_BLOCK - MLA_ROPE - MLA_NOPE), w_uq.dtype)
    wq1 = jnp.concatenate([x1, x2, nope, zpad], axis=-1).reshape(MLA_Q_RANK, MLA_HEADS * HEAD_BLOCK)
    wq2 = jnp.concatenate([x2, x1, jnp.zeros_like(nope), zpad], axis=-1).reshape(MLA_Q_RANK,
                                                                                   MLA_HEADS * HEAD_BLOCK)
    zk = jnp.zeros((MLA_KV_RANK, MLA_HEADS, MLA_ROPE), w_uk.dtype)
    wk = jnp.concatenate([zk, w_uk, zk], axis=-1).reshape(MLA_KV_RANK, MLA_HEADS * HEAD_BLOCK)
    wv = w_uv.reshape(MLA_KV_RANK, MLA_HEADS * MLA_V)
    bf = pad(fox_forget_bias.reshape(1, FOX_HEADS), LANES)
    wkvt = jnp.concatenate([fk, fv], axis=1).T
    return wa.astype(BF16), wg.astype(BF16), wq1.astype(BF16), wq2.astype(BF16), wk.astype(BF16), \
        wv.astype(BF16), bf, wkvt.astype(BF16)


def kernel(x_prompt, x_sample, cache_mla_ckv, cache_mla_krope, cache_fox_k, cache_fox_v, cache_fox_logf,
           page_table, p_prompt, p_sample, norm1, w_in, q_norm, kv_norm, w_uq, w_uk, w_uv, fox_forget_bias,
           w_up_a, w_up_b, w_o, w_ple, w_ple_gate, final_norm):
    depth = norm1.shape[0]
    assert depth == 1, "single-layer step"
    batch, seq, _ = x_prompt.shape
    nb, dec_seq, _ = x_sample.shape
    assert dec_seq == 1, "one new token per decode sequence"
    n_pool, page = cache_mla_ckv.shape[1], cache_mla_ckv.shape[2]
    n_pages = page_table.shape[1]
    past_len = n_pages * page
    assert page == LANES and n_pages % PAGES_PER_STEP == 0

    wa, wg, wq1, wq2, wk, wv, bf, wkvt = _pack_weights(w_in[0], w_uq[0], w_uk[0], w_uv[0], fox_forget_bias[0])
    n1 = norm1[0].reshape(1, D_MODEL)
    qn = q_norm[0].reshape(1, MLA_Q_RANK)
    kvn = kv_norm[0].reshape(1, MLA_KV_RANK)
    fn = final_norm.reshape(1, D_MODEL)
    wua, wub, wo = w_up_a[0].astype(BF16), w_up_b[0].astype(BF16), w_o[0].astype(BF16)
    wple, wpg = w_ple[0].astype(BF16), w_ple_gate[0].astype(BF16)

    n = batch * seq
    x2 = x_prompt.reshape(n, D_MODEL)
    (q_full, k_full, vt_mla, ckv_p, kr_p, fq_b, fk_b, fvt_b, fkt_p, fvt_p, lf_p, ccol) = _proj_prompt(
        x2, n1, wa, qn, kvn, wq1, wq2, wk, wv.T, bf, wkvt, batch=batch, seq=seq, tm=256)
    o_a = _flash(q_full, k_full, vt_mla, None, batch=batch, seq=seq, tq=512, fox=False)
    o_b = _flash(fq_b, fk_b, fvt_b, ccol, batch=batch, seq=seq, tq=512, fox=True)
    to_heads = lambda a: jnp.transpose(a.reshape(batch, FOX_HEADS, FOX_HD, seq), (0, 3, 1, 2))[None]
    fk_p, fv_p = to_heads(fkt_p), to_heads(fvt_p)
    y_p = _merge(x2, p_prompt[0].reshape(n, PLE_DIM), o_a, o_b, n1, wg, wua, wub, wo, wple, wpg, fn, tm=256)

    xs = x_sample.reshape(nb, D_MODEL)
    ckv_s, kr_s, fq_s, fk_s, fv_s, lf_s, qcat = _proj_sample(xs, n1, wa, qn, kvn, wq1, wq2, wk, bf,
                                                              past_len=past_len)
    page_flat = page_table.reshape(nb * n_pages)
    lf_cols = jnp.broadcast_to(lf_s[:, :FOX_HEADS, None], (nb, FOX_HEADS, LANES))
    kc = jnp.transpose(cache_fox_k[0], (0, 2, 3, 1)).reshape(n_pool, FOX_WIDTH, page)
    vc = jnp.transpose(cache_fox_v[0], (0, 2, 3, 1)).reshape(n_pool, FOX_WIDTH, page)
    krc = jnp.transpose(cache_mla_krope[0], (0, 2, 1))
    lfc = jnp.transpose(cache_fox_logf[0], (0, 2, 1))
    oa_s, ob_s = _decode(
        page_flat, fq_s.reshape(nb, 1, FOX_WIDTH), jnp.transpose(qcat, (1, 0, 2)),
        fk_s.reshape(nb, 1, FOX_WIDTH), fv_s.reshape(nb, 1, FOX_WIDTH), lf_cols,
        ckv_s.reshape(nb, 1, MLA_KV_RANK), kr_s.reshape(nb, 1, MLA_ROPE), wv,
        kc, vc, cache_mla_ckv[0], krc, lfc, nb=nb, n_pages=n_pages, page=page)
    y_s = _merge(xs, p_sample[0].reshape(nb, PLE_DIM), oa_s.reshape(nb, 512), ob_s.reshape(nb, 512),
                 n1, wg, wua, wub, wo, wple, wpg, fn, tm=nb)

    return (y_p.reshape(batch, seq, D_MODEL), y_s.reshape(nb, 1, D_MODEL),
            ckv_p.reshape(1, batch, seq, MLA_KV_RANK), kr_p.reshape(1, batch, seq, MLA_ROPE),
            fk_p, fv_p, lf_p.reshape(1, batch, seq, FOX_HEADS),
            ckv_s.reshape(1, nb, 1, MLA_KV_RANK), kr_s.reshape(1, nb, 1, MLA_ROPE),
            fk_s.reshape(1, nb, 1, FOX_HEADS, FOX_HD), fv_s.reshape(1, nb, 1, FOX_HEADS, FOX_HD),
            lf_s[:, :FOX_HEADS].reshape(1, nb, 1, FOX_HEADS))
```
